```python
import math
import jax, jax.numpy as jnp
from jax import lax
import numpy as np

D_MODEL = 4096
BATCH = 1
SEQ = 8192
DEPTH = 4

N_MIXERS = 2
N_ATTN_LAYERS = (DEPTH + 1) // 2
N_SSD_LAYERS = DEPTH // 2
SB_HEAD_DIM = 128
SB_HEADS = D_MODEL // SB_HEAD_DIM
Q_BLOCK = 128
SSD_EXPAND = 2
SSD_D_INNER = SSD_EXPAND * D_MODEL
SSD_HEAD_DIM = 64
SSD_HEADS = SSD_D_INNER // SSD_HEAD_DIM
SSD_GROUPS = 8
SSD_STATE = 128
SSD_CONV = 4
SSD_CHUNK = 128
SSD_CONV_DIM = SSD_D_INNER + 2 * SSD_GROUPS * SSD_STATE
SSD_IN_DIM = SSD_D_INNER + SSD_CONV_DIM + SSD_HEADS
N_EXPERTS = 32
TOP_K = 4
EXPERT_FF = D_MODEL // 16
SWIGLU_ALPHA = 1.702
SWIGLU_LIMIT = 7.0
COND_RANK = D_MODEL // 8
N_MOD = 6
NORM_EPS = 1e-6

kernel_name = "hybrid_stickbreak_ssd_moe_adaln"


def rms_norm(x, g):
    xf = x.astype(jnp.float32)
    y = xf * lax.rsqrt(jnp.mean(xf * xf, axis=-1, keepdims=True) + NORM_EPS)
    return (y * g.astype(jnp.float32)).astype(x.dtype)


def stick_breaking_attention(h, w_in, w_out):
    bsz, L, _ = h.shape
    qkv = h @ w_in
    q, k, v = jnp.split(qkv, 3, axis=-1)

    def heads(t):
        return t.reshape(bsz, L, SB_HEADS, SB_HEAD_DIM).transpose(0, 2, 1, 3).astype(jnp.float32)

    q = heads(q) * (SB_HEAD_DIM ** -0.5)
    k = heads(k)
    v = heads(v)
    n_blk = L // Q_BLOCK
    q_blocks = q.reshape(bsz, SB_HEADS, n_blk, Q_BLOCK, SB_HEAD_DIM).transpose(2, 0, 1, 3, 4)
    k_pos = jnp.arange(L)

    def block(args):
        qb, bi = args
        q_pos = bi * Q_BLOCK + jnp.arange(Q_BLOCK)
        logits = jnp.einsum("bhqd,bhkd->bhqk", qb, k)
        visible = k_pos[None, :] < q_pos[:, None]
        log_keep = jnp.where(visible, jax.nn.log_sigmoid(-logits), 0.0)
        log_after = lax.cumsum(log_keep, axis=3, reverse=True) - log_keep
        log_w = jax.nn.log_sigmoid(logits) + log_after
        w = jnp.where(visible, jnp.exp(log_w), 0.0)
        return jnp.einsum("bhqk,bhkd->bhqd", w, v)

    out = lax.map(block, (q_blocks, jnp.arange(n_blk)))
    out = out.transpose(1, 0, 3, 2, 4).reshape(bsz, L, D_MODEL).astype(h.dtype)
    return out @ w_out


def causal_depthwise_conv(u, w, b):
    ch = u.shape[-1]
    y = lax.conv_general_dilated(u, w[:, None, :].astype(u.dtype), window_strides=(1,),
                                 padding=[(SSD_CONV - 1, 0)],
                                 dimension_numbers=("NWC", "WIO", "NWC"),
                                 feature_group_count=ch)
    return y + b


def ssd_chunked_scan(xh, dt, a, bm, cm):
    bsz, L, nh, hp = xh.shape
    ng, ns = bm.shape[2], bm.shape[3]
    nr = nh // ng
    nc = L // SSD_CHUNK
    X = (xh * dt[..., None]).reshape(bsz, nc, SSD_CHUNK, ng, nr, hp)
    a_dt = (dt * a).reshape(bsz, nc, SSD_CHUNK, ng, nr)
    acs = jnp.cumsum(a_dt, axis=2).transpose(0, 1, 3, 4, 2)
    Bc = bm.reshape(bsz, nc, SSD_CHUNK, ng, ns)
    Cc = cm.reshape(bsz, nc, SSD_CHUNK, ng, ns)
    seg = acs[..., :, None] - acs[..., None, :]
    causal = jnp.tril(jnp.ones((SSD_CHUNK, SSD_CHUNK), dtype=bool))
    lmat = jnp.exp(jnp.where(causal, seg, -jnp.inf))
    cb = jnp.einsum("bclgn,bcsgn->bcgls", Cc, Bc)
    scores = cb[:, :, :, None] * lmat
    y_diag = jnp.einsum("bcgrls,bcsgrp->bclgrp", scores, X)
    decay_states = jnp.exp(acs[..., -1:] - acs)
    states = jnp.einsum("bclgn,bcgrl,bclgrp->bcgrpn", Bc, decay_states, X)
    chunk_decay = jnp.exp(acs[..., -1])

    def step(carry, inp):
        dec, st = inp
        return carry * dec[..., None, None] + st, carry

    init = jnp.zeros_like(states[:, 0])
    _, prev = lax.scan(step, init, (jnp.moveaxis(chunk_decay, 1, 0), jnp.moveaxis(states, 1, 0)))
    prev = jnp.moveaxis(prev, 0, 1)
    y_off = jnp.einsum("bclgn,bcgrpn,bcgrl->bclgrp", Cc, prev, jnp.exp(acs))
    return (y_diag + y_off).reshape(bsz, L, nh, hp)


def gated_rms_norm(y, z, w):
    yg = y * jax.nn.silu(z.astype(jnp.float32))
    shp = yg.shape
    yg = yg.reshape(shp[:-1] + (SSD_GROUPS, shp[-1] // SSD_GROUPS))
    yg = yg * lax.rsqrt(jnp.mean(yg * yg, axis=-1, keepdims=True) + NORM_EPS)
    return yg.reshape(shp) * w.astype(jnp.float32)


def ssd_mixer(h, w_in, conv_w, conv_b, dt_bias, a_log, d_skip, norm_w, w_out):
    bsz, L, _ = h.shape
    proj = h @ w_in
    z = proj[..., :SSD_D_INNER]
    xbc = proj[..., SSD_D_INNER:SSD_D_INNER + SSD_CONV_DIM]
    dt_raw = proj[..., SSD_D_INNER + SSD_CONV_DIM:]
    xbc = jax.nn.silu(causal_depthwise_conv(xbc, conv_w, conv_b)).astype(jnp.float32)
    gn = SSD_GROUPS * SSD_STATE
    xs = xbc[..., :SSD_D_INNER].reshape(bsz, L, SSD_HEADS, SSD_HEAD_DIM)
    bm = xbc[..., SSD_D_INNER:SSD_D_INNER + gn].reshape(bsz, L, SSD_GROUPS, SSD_STATE)
    cm = xbc[..., SSD_D_INNER + gn:].reshape(bsz, L, SSD_GROUPS, SSD_STATE)
    dt = jax.nn.softplus(dt_raw.astype(jnp.float32) + dt_bias.astype(jnp.float32))
    a = -jnp.exp(a_log.astype(jnp.float32))
    y = ssd_chunked_scan(xs, dt, a, bm, cm)
    y = y + d_skip.astype(jnp.float32)[:, None] * xs
    y = gated_rms_norm(y.reshape(bsz, L, SSD_D_INNER), z, norm_w)
    return y.astype(h.dtype) @ w_out


def moe_ffn(h, w_router, b_router, w_gu, b_gu, w_down, b_down):
    bsz, L, dm = h.shape
    t = h.reshape(-1, dm)
    logits = (t @ w_router + b_router).astype(jnp.float32)
    top_vals, top_idx = lax.top_k(logits, TOP_K)
    top_w = jax.nn.softmax(top_vals, axis=-1)
    comb = jnp.sum(jax.nn.one_hot(top_idx, N_EXPERTS, dtype=jnp.float32) * top_w[..., None], axis=1)
    comb = comb.astype(h.dtype)
    gu = jnp.einsum("td,edf->tef", t, w_gu) + b_gu
    x_glu = jnp.minimum(gu[..., 0::2], SWIGLU_LIMIT)
    x_lin = jnp.clip(gu[..., 1::2], -SWIGLU_LIMIT, SWIGLU_LIMIT)
    act = x_glu * jax.nn.sigmoid(SWIGLU_ALPHA * x_glu) * (x_lin + 1.0)
    y = jnp.einsum("tef,efd->td", act * comb[:, :, None], w_down) + comb @ b_down
    return y.reshape(bsz, L, dm)


def setup_inputs(seed: int = 0) -> dict:
    key = jax.random.key(seed)
    ks = jax.random.split(key, 32)

    def nrm(k, shape, scale):
        return jax.random.normal(k, shape, jnp.float32) * scale

    dt0 = jnp.exp(jax.random.uniform(ks[17], (N_SSD_LAYERS, SSD_HEADS), jnp.float32)
                  * (math.log(0.1) - math.log(0.001)) + math.log(0.001))
    return {
        "x": nrm(ks[0], (BATCH, SEQ, D_MODEL), 1.0),
        "c": nrm(ks[1], (BATCH, D_MODEL), 1.0),
        "w_cond": nrm(ks[2], (D_MODEL, COND_RANK), D_MODEL ** -0.5),
        "b_cond": nrm(ks[3], (COND_RANK,), 0.02),
        "w_mod": nrm(ks[4], (DEPTH, COND_RANK, N_MOD * D_MODEL), 0.5 * COND_RANK ** -0.5),
        "b_mod": nrm(ks[5], (DEPTH, N_MOD * D_MODEL), 0.02),
        "g_mix_pre": 1.0 + nrm(ks[6], (DEPTH, D_MODEL), 0.02),
        "g_mix_post": 1.0 + nrm(ks[7], (DEPTH, D_MODEL), 0.02),
        "g_ffn_pre": 1.0 + nrm(ks[8], (DEPTH, D_MODEL), 0.02),
        "g_ffn_post": 1.0 + nrm(ks[9], (DEPTH, D_MODEL), 0.02),
        "attn_w_in": nrm(ks[10], (N_ATTN_LAYERS, D_MODEL, 3 * D_MODEL), D_MODEL ** -0.5),
        "attn_w_out": nrm(ks[11], (N_ATTN_LAYERS, D_MODEL, D_MODEL), D_MODEL ** -0.5),
        "ssd_w_in": nrm(ks[12], (N_SSD_LAYERS, D_MODEL, SSD_IN_DIM), D_MODEL ** -0.5),
        "ssd_conv_w": nrm(ks[13], (N_SSD_LAYERS, SSD_CONV, SSD_CONV_DIM), SSD_CONV ** -0.5),
        "ssd_conv_b": nrm(ks[14], (N_SSD_LAYERS, SSD_CONV_DIM), 0.02),
        "ssd_dt_bias": dt0 + jnp.log(-jnp.expm1(-dt0)),
        "ssd_a_log": jnp.log(jax.random.uniform(ks[15], (N_SSD_LAYERS, SSD_HEADS), jnp.float32, 1.0, 16.0)),
        "ssd_d": 1.0 + nrm(ks[16], (N_SSD_LAYERS, SSD_HEADS), 0.1),
        "ssd_norm_w": 1.0 + nrm(ks[18], (N_SSD_LAYERS, SSD_D_INNER), 0.02),
        "ssd_w_out": nrm(ks[19], (N_SSD_LAYERS, SSD_D_INNER, D_MODEL), SSD_D_INNER ** -0.5),
        "moe_w_router": nrm(ks[20], (DEPTH, D_MODEL, N_EXPERTS), D_MODEL ** -0.5),
        "moe_b_router": nrm(ks[21], (DEPTH, N_EXPERTS), 0.01),
        "moe_w_gu": nrm(ks[22], (DEPTH, N_EXPERTS, D_MODEL, 2 * EXPERT_FF), D_MODEL ** -0.5),
        "moe_b_gu": nrm(ks[23], (DEPTH, N_EXPERTS, 2 * EXPERT_FF), 0.02),
        "moe_w_down": nrm(ks[24], (DEPTH, N_EXPERTS, EXPERT_FF, D_MODEL), EXPERT_FF ** -0.5),
        "moe_b_down": nrm(ks[25], (DEPTH, N_EXPERTS, D_MODEL), 0.02),
    }


def reference(x, c, w_cond, b_cond, w_mod, b_mod, g_mix_pre, g_mix_post, g_ffn_pre, g_ffn_post,
              attn_w_in, attn_w_out, ssd_w_in, ssd_conv_w, ssd_conv_b, ssd_dt_bias, ssd_a_log,
              ssd_d, ssd_norm_w, ssd_w_out, moe_w_router, moe_b_router, moe_w_gu, moe_b_gu,
              moe_w_down, moe_b_down):
    cond = jax.nn.silu(jax.nn.silu(c) @ w_cond + b_cond)
    for i in range(DEPTH):
        mod = cond @ w_mod[i] + b_mod[i]
        sh_m, sc_m, ga_m, sh_f, sc_f, ga_f = [m[:, None, :] for m in jnp.split(mod, N_MOD, axis=-1)]
        h = rms_norm(x, g_mix_pre[i]) * (1.0 + sc_m) + sh_m
        j = i // N_MIXERS
        if i % N_MIXERS == 0:
            y = stick_breaking_attention(h, attn_w_in[j], attn_w_out[j])
        else:
            y = ssd_mixer(h, ssd_w_in[j], ssd_conv_w[j], ssd_conv_b[j], ssd_dt_bias[j],
                          ssd_a_log[j], ssd_d[j], ssd_norm_w[j], ssd_w_out[j])
        x = x + ga_m * rms_norm(y, g_mix_post[i])
        h = rms_norm(x, g_ffn_pre[i]) * (1.0 + sc_f) + sh_f
        y = moe_ffn(h, moe_w_router[i], moe_b_router[i], moe_w_gu[i], moe_b_gu[i],
                    moe_w_down[i], moe_b_down[i])
        x = x + ga_f * rms_norm(y, g_ffn_post[i])
    return x
```

```python
import functools

import jax
import jax.numpy as jnp
from jax import lax
from jax.experimental import pallas as pl
from jax.experimental.pallas import tpu as pltpu

SB_HEAD_DIM = 128
SSD_HEAD_DIM = 64
SSD_GROUPS = 8
SSD_STATE = 128
SSD_CONV = 4
SSD_CHUNK = 128
TOP_K = 4
SWIGLU_ALPHA = 1.702
SWIGLU_LIMIT = 7.0
N_MOD = 6
NORM_EPS = 1e-6

LANES = 128
SUBLANES = 8
VMEM_LIMIT_BYTES = 56 * 1024 * 1024

F32 = jnp.float32
BF16 = jnp.bfloat16
HIGHEST = lax.Precision.HIGHEST
NEG_INF = float("-inf")


def _params(sem):
    return pltpu.CompilerParams(dimension_semantics=sem, vmem_limit_bytes=VMEM_LIMIT_BYTES)


def _sigmoid(x):
    return 1.0 / (1.0 + jnp.exp(-x))


def _softplus(x):
    return jnp.maximum(x, 0.0) + jnp.log(1.0 + jnp.exp(-jnp.abs(x)))


def _tile(n, pref):
    if n <= pref:
        return n
    t = pref
    while n % t:
        t //= 2
    return t


def _cond_kernel(c_ref, w_ref, b_ref, o_ref):
    c = c_ref[...]
    s = c * _sigmoid(c)
    r = jnp.dot(s, w_ref[...], precision=HIGHEST, preferred_element_type=F32) + b_ref[...]
    o_ref[...] = r * _sigmoid(r)


def _mod_kernel(cond_ref, w_ref, b_ref, o_ref):
    o_ref[0] = jnp.dot(cond_ref[...], w_ref[0], precision=HIGHEST,
                       preferred_element_type=F32) + b_ref[0]


def _conditioning(c, w_cond, b_cond, w_mod, b_mod):
    bsz, d = c.shape
    rank = w_cond.shape[1]
    depth, _, nmod = w_mod.shape
    rows = SUBLANES
    c8 = jnp.broadcast_to(c[:1], (rows, d))
    cond = pl.pallas_call(
        _cond_kernel,
        out_shape=jax.ShapeDtypeStruct((rows, rank), F32),
        compiler_params=_params(None),
        name="cond_trunk",
    )(c8, w_cond, b_cond.reshape(1, rank))
    tn = _tile(nmod, 4096)
    mod = pl.pallas_call(
        _mod_kernel,
        grid=(depth, nmod // tn),
        in_specs=[
            pl.BlockSpec((rows, rank), lambda i, j: (0, 0)),
            pl.BlockSpec((1, rank, tn), lambda i, j: (i, 0, j)),
            pl.BlockSpec((1, 1, tn), lambda i, j: (i, 0, j)),
        ],
        out_specs=pl.BlockSpec((1, rows, tn), lambda i, j: (i, 0, j)),
        out_shape=jax.ShapeDtypeStruct((depth, rows, nmod), F32),
        compiler_params=_params(("arbitrary", "arbitrary")),
        name="cond_mod",
    )(cond, w_mod, b_mod.reshape(depth, 1, nmod))
    return mod[:, 0, :]


def _rms(y):
    return y * lax.rsqrt(jnp.mean(y * y, axis=-1, keepdims=True) + NORM_EPS)


def _glue_kernel(*refs, has_resid, has_next, has_router):
    it = iter(refs)
    x_ref = next(it)
    if has_resid:
        y_ref, ga_ref, gpost_ref = next(it), next(it), next(it)
    if has_next:
        gpre_ref, sc_ref, sh_ref = next(it), next(it), next(it)
    if has_router:
        wr_ref, br_ref = next(it), next(it)
    if has_resid:
        xo_ref = next(it)
    if has_next:
        h_ref = next(it)
    if has_router:
        comb_ref = next(it)

    x = x_ref[...]
    if has_resid:
        x = x + ga_ref[...] * (_rms(y_ref[...]) * gpost_ref[...])
        xo_ref[...] = x
    if has_next:
        h = (_rms(x) * gpre_ref[...]) * (1.0 + sc_ref[...]) + sh_ref[...]
        h_ref[...] = h.astype(h_ref.dtype)
    if has_router:
        logits = jnp.dot(h, wr_ref[...], precision=HIGHEST,
                         preferred_element_type=F32) + br_ref[...]
        lane = lax.broadcasted_iota(jnp.int32, logits.shape, 1).astype(F32)
        work = logits
        vals, sels = [], []
        for _ in range(TOP_K):
            m = jnp.max(work, axis=1, keepdims=True)
            idx = jnp.min(jnp.where(work == m, lane, float(LANES)), axis=1, keepdims=True)
            sel = lane == idx
            vals.append(m)
            sels.append(sel)
            work = jnp.where(sel, NEG_INF, work)
        exps = [jnp.exp(v - vals[0]) for v in vals]
        denom = exps[0]
        for e in exps[1:]:
            denom = denom + e
        comb = jnp.zeros_like(logits)
        for sel, e in zip(sels, exps):
            comb = jnp.where(sel, e / denom, comb)
        comb_ref[...] = comb


def _glue(x, resid=None, nxt=None, router=None, tm=128):
    t, d = x.shape
    tm = _tile(t, tm)
    row = pl.BlockSpec((tm, d), lambda i: (i, 0))
    vec = pl.BlockSpec((1, d), lambda i: (0, 0))
    args, in_specs, out_shape, out_specs = [x], [row], [], []
    if resid is not None:
        y, ga, gpost = resid
        args += [y, ga.reshape(1, d), gpost.reshape(1, d)]
        in_specs += [row, vec, vec]
        out_shape.append(jax.ShapeDtypeStruct((t, d), F32))
        out_specs.append(row)
    if nxt is not None:
        gpre, sc, sh = nxt
        args += [gpre.reshape(1, d), sc.reshape(1, d), sh.reshape(1, d)]
        in_specs += [vec, vec, vec]
        out_shape.append(jax.ShapeDtypeStruct((t, d), BF16))
        out_specs.append(row)
    if router is not None:
        wr, br = router
        args += [wr, br]
        in_specs += [pl.BlockSpec((d, LANES), lambda i: (0, 0)),
                     pl.BlockSpec((1, LANES), lambda i: (0, 0))]
        out_shape.append(jax.ShapeDtypeStruct((t, LANES), F32))
        out_specs.append(pl.BlockSpec((tm, LANES), lambda i: (i, 0)))
    outs = pl.pallas_call(
        functools.partial(_glue_kernel, has_resid=resid is not None,
                          has_next=nxt is not None, has_router=router is not None),
        grid=(t // tm,),
        in_specs=in_specs,
        out_specs=out_specs,
        out_shape=out_shape,
        compiler_params=_params(("arbitrary",)),
        name="glue",
    )(*args)
    return outs


def _mm_kernel(*refs, nk, scale, has_bias):
    if has_bias:
        a_ref, w_ref, cmb_ref, bd_ref, o_ref = refs[:5]
        rest = refs[5:]
    else:
        a_ref, w_ref, o_ref = refs[:3]
        rest = refs[3:]
    part = jnp.dot(a_ref[...], w_ref[...], preferred_element_type=F32)
    if nk == 1:
        if has_bias:
            part = part + jnp.dot(cmb_ref[...], bd_ref[...], precision=HIGHEST,
                                  preferred_element_type=F32)
        if scale != 1.0:
            part = part * scale
        o_ref[...] = part.astype(o_ref.dtype)
        return
    acc_ref = rest[0] if rest else o_ref
    k = pl.program_id(2)

    @pl.when(k == 0)
    def _():
        init = part
        if has_bias:
            init = init + jnp.dot(cmb_ref[...], bd_ref[...], precision=HIGHEST,
                                  preferred_element_type=F32)
        acc_ref[...] = init

    @pl.when(k > 0)
    def _():
        acc_ref[...] += part

    if rest or scale != 1.0:
        @pl.when(k == nk - 1)
        def _():
            r = acc_ref[...]
            if scale != 1.0:
                r = r * scale
            o_ref[...] = r.astype(o_ref.dtype)


def _matmul(a, w, out_dtype, scale=1.0, bias=None, tm=1024, tn=1024, tk=4096):
    m, k = a.shape
    _, n = w.shape
    tm, tn, tk = _tile(m, tm), _tile(n, tn), _tile(k, tk)
    nk = k // tk
    args = [a, w]
    in_specs = [pl.BlockSpec((tm, tk), lambda i, j, kk: (i, kk)),
                pl.BlockSpec((tk, tn), lambda i, j, kk: (kk, j))]
    if bias is not None:
        args += list(bias)
        in_specs += [pl.BlockSpec((tm, LANES), lambda i, j, kk: (i, 0)),
                     pl.BlockSpec((LANES, tn), lambda i, j, kk: (0, j))]
    scratch = []
    if nk > 1 and out_dtype != F32:
        scratch = [pltpu.VMEM((tm, tn), F32)]
    return pl.pallas_call(
        functools.partial(_mm_kernel, nk=nk, scale=scale, has_bias=bias is not None),
        grid=(m // tm, n // tn, nk),
        in_specs=in_specs,
        out_specs=pl.BlockSpec((tm, tn), lambda i, j, kk: (i, j)),
        out_shape=jax.ShapeDtypeStruct((m, n), out_dtype),
        scratch_shapes=scratch,
        compiler_params=_params(("arbitrary", "arbitrary", "arbitrary")),
        name="matmul",
    )(*args)


def _attn_kernel(q_ref, k_ref, v_ref, o_ref, *, tq, tk):
    i = pl.program_id(1)
    q = q_ref[...]
    hd = q.shape[1]
    row = lax.broadcasted_iota(jnp.int32, (tq, tk), 0)
    col = lax.broadcasted_iota(jnp.int32, (tq, tk), 1)
    r2 = lax.broadcasted_iota(jnp.int32, (tk, tk), 0)
    c2 = lax.broadcasted_iota(jnp.int32, (tk, tk), 1)
    later = jnp.where(r2 > c2, 1.0, 0.0).astype(BF16)

    def tile(k_start, carry, acc, diag_off):
        kk = k_ref[pl.ds(k_start, tk), :]
        vv = v_ref[pl.ds(k_start, tk), :]
        z = lax.dot_general(q, kk, (((1,), (1,)), ((), ())), preferred_element_type=F32)
        sp = _softplus(z)
        if diag_off is not None:
            vis = (col + diag_off) < row
            lk = jnp.where(vis, -sp, 0.0)
        else:
            lk = -sp
        lk_hi = lk.astype(BF16)
        lk_lo = (lk - lk_hi.astype(F32)).astype(BF16)
        la = (jnp.dot(lk_hi, later, preferred_element_type=F32)
              + jnp.dot(lk_lo, later, preferred_element_type=F32))
        w = jnp.exp(z - sp + la + carry)
        if diag_off is not None:
            w = jnp.where(vis, w, 0.0)
        acc = acc + jnp.dot(w.astype(BF16), vv, preferred_element_type=F32)
        carry = carry + la[:, 0:1] + lk[:, 0:1]
        return carry, acc

    carry = jnp.zeros((tq, 1), F32)
    acc = jnp.zeros((tq, hd), F32)
    q0 = i * tq
    for d in range(tq // tk - 1, -1, -1):
        carry, acc = tile(pl.multiple_of(q0 + d * tk, tk), carry, acc, d * tk)

    nfull = q0 // tk

    def body(t, ca):
        k_start = pl.multiple_of((nfull - 1 - t) * tk, tk)
        return tile(k_start, ca[0], ca[1], None)

    carry, acc = lax.fori_loop(0, nfull, body, (carry, acc))
    o_ref[...] = acc.astype(o_ref.dtype)


def _attention(q, kv, tq=256, tk=256):
    l, d = q.shape
    nh = d // SB_HEAD_DIM
    tq = _tile(l, tq)
    tk = _tile(tq, tk)
    return pl.pallas_call(
        functools.partial(_attn_kernel, tq=tq, tk=tk),
        grid=(nh, l // tq),
        in_specs=[
            pl.BlockSpec((tq, SB_HEAD_DIM), lambda h, i: (i, h)),
            pl.BlockSpec((l, SB_HEAD_DIM), lambda h, i: (0, h)),
            pl.BlockSpec((l, SB_HEAD_DIM), lambda h, i: (0, nh + h)),
        ],
        out_specs=pl.BlockSpec((tq, SB_HEAD_DIM), lambda h, i: (i, h)),
        out_shape=jax.ShapeDtypeStruct((l, d), BF16),
        compiler_params=_params(("arbitrary", "arbitrary")),
        name="sb_attention",
    )(q, kv, kv)


def _conv_kernel(u_ref, w_ref, b_ref, o_ref, ext_ref, *, tl):
    halo = SUBLANES

    @pl.when(pl.program_id(1) == 0)
    def _():
        ext_ref[0:halo, :] = jnp.zeros((halo, ext_ref.shape[1]), F32)

    u = u_ref[...]
    ext_ref[halo:halo + tl, :] = u
    acc = b_ref[...] + w_ref[SSD_CONV - 1:SSD_CONV, :] * u
    for k in range(SSD_CONV - 1):
        off = halo - (SSD_CONV - 1) + k
        acc = acc + w_ref[k:k + 1, :] * ext_ref[off:off + tl, :]
    o_ref[...] = acc * _sigmoid(acc)
    ext_ref[0:halo, :] = u[tl - halo:tl, :]


def _conv_silu(u, w, b, tl=512, tc=1024):
    l, c = u.shape
    tl, tc = _tile(l, tl), _tile(c, tc)
    return pl.pallas_call(
        functools.partial(_conv_kernel, tl=tl),
        grid=(c // tc, l // tl),
        in_specs=[
            pl.BlockSpec((tl, tc), lambda j, i: (i, j)),
            pl.BlockSpec((SSD_CONV, tc), lambda j, i: (0, j)),
            pl.BlockSpec((1, tc), lambda j, i: (0, j)),
        ],
        out_specs=pl.BlockSpec((tl, tc), lambda j, i: (i, j)),
        out_shape=jax.ShapeDtypeStruct((l, c), F32),
        scratch_shapes=[pltpu.VMEM((tl + SUBLANES, tc), F32)],
        compiler_params=_params(("arbitrary", "arbitrary")),
        name="ssd_conv",
    )(u, w, b.reshape(1, c))


def _split3(x):
    p1 = x.astype(BF16)
    r1 = x - p1.astype(F32)
    p2 = r1.astype(BF16)
    p3 = (r1 - p2.astype(F32)).astype(BF16)
    return p1, p2, p3


def _dtprep_kernel(dt_ref, bias_ref, alog_ref, dtr_ref, acsr_ref, acsc_ref, *, hpg):
    dt = _softplus(dt_ref[...] + bias_ref[...])
    adt = dt * (-jnp.exp(alog_ref[...]))
    n = dt.shape[0]
    r = lax.broadcasted_iota(jnp.int32, (n, n), 0)
    c = lax.broadcasted_iota(jnp.int32, (n, n), 1)
    upto = jnp.where(c <= r, 1.0, 0.0).astype(BF16)
    acs = sum(jnp.dot(upto, p, preferred_element_type=F32) for p in _split3(adt))
    dtr_ref[...] = dt.T
    acsr_ref[...] = acs.T
    for g in range(acsc_ref.shape[0]):
        acsc_ref[g] = acs[:, g * hpg:(g + 1) * hpg]


def _dtprep(dt_raw, dt_bias, a_log):
    l, nh = dt_raw.shape
    hpg = nh // SSD_GROUPS
    ch = SSD_CHUNK
    return pl.pallas_call(
        functools.partial(_dtprep_kernel, hpg=hpg),
        grid=(l // ch,),
        in_specs=[
            pl.BlockSpec((ch, nh), lambda i: (i, 0)),
            pl.BlockSpec((1, nh), lambda i: (0, 0)),
            pl.BlockSpec((1, nh), lambda i: (0, 0)),
        ],
        out_specs=[
            pl.BlockSpec((nh, ch), lambda i: (0, i)),
            pl.BlockSpec((nh, ch), lambda i: (0, i)),
            pl.BlockSpec((SSD_GROUPS, ch, hpg), lambda i: (0, i, 0)),
        ],
        out_shape=[
            jax.ShapeDtypeStruct((nh, l), F32),
            jax.ShapeDtypeStruct((nh, l), F32),
            jax.ShapeDtypeStruct((SSD_GROUPS, l, hpg), F32),
        ],
        compiler_params=_params(("arbitrary",)),
        name="ssd_dtprep",
    )(dt_raw, dt_bias.reshape(1, nh), a_log.reshape(1, nh))


def _ssd_kernel(xs_ref, b_ref, c_ref, dtr_ref, acsr_ref, acsc_ref, z_ref, dexp_ref, nw_ref,
                o_ref, st_ref, y_ref, *, npairs):
    ch = SSD_CHUNK
    hp = SSD_HEAD_DIM

    @pl.when(pl.program_id(1) == 0)
    def _():
        st_ref[...] = jnp.zeros(st_ref.shape, F32)

    bm = b_ref[...]
    cm = c_ref[...]
    cb = lax.dot_general(cm.astype(BF16), bm.astype(BF16), (((1,), (1,)), ((), ())),
                         preferred_element_type=F32)
    bt = bm.T
    row = lax.broadcasted_iota(jnp.int32, (ch, ch), 0)
    col = lax.broadcasted_iota(jnp.int32, (ch, ch), 1)
    causal = col <= row
    first = lax.broadcasted_iota(jnp.int32, (ch, 2 * hp), 1) < hp
    dtr = dtr_ref[...]
    acsr = acsr_ref[...]
    acsc = acsc_ref[0]

    for pr in range(npairs):
        sl = slice(pr * 2 * hp, (pr + 1) * 2 * hp)
        xs_b = xs_ref[:, sl].astype(BF16)
        prev = st_ref[pr]
        rhs = jnp.concatenate([xs_b, prev.astype(BF16)], axis=0)
        outs, news = [], []
        for hh in (2 * pr, 2 * pr + 1):
            a_col = jnp.broadcast_to(acsc[:, hh:hh + 1], (ch, ch))
            a_row = acsr[hh:hh + 1, :]
            dt_row = dtr[hh:hh + 1, :]
            lmat = jnp.exp(jnp.where(causal, a_col - a_row, NEG_INF))
            scores = cb * lmat * dt_row
            c_dec = cm * jnp.exp(a_col)
            lhs = jnp.concatenate([scores.astype(BF16), c_dec.astype(BF16)], axis=1)
            outs.append(jnp.dot(lhs, rhs, preferred_element_type=F32))
            a_last = a_row[:, ch - 1:ch]
            b_dec = (bt * (dt_row * jnp.exp(a_last - a_row))).astype(BF16)
            news.append(jnp.exp(a_last) * prev
                        + jnp.dot(b_dec, xs_b, preferred_element_type=F32))
        y_ref[:, sl] = jnp.where(first, outs[0], outs[1])
        st_ref[pr] = jnp.where(first, news[0], news[1])

    y = y_ref[...] + dexp_ref[...] * xs_ref[...]
    z = z_ref[...]
    yg = y * (z * _sigmoid(z))
    o_ref[...] = (_rms(yg) * nw_ref[...]).astype(o_ref.dtype)


def _ssd_scan(xbc, z, dtr, acsr, acsc, d_exp, norm_w):
    l, d_inner = z.shape
    gw = d_inner // SSD_GROUPS
    npairs = gw // (2 * SSD_HEAD_DIM)
    hpg = gw // SSD_HEAD_DIM
    ch = SSD_CHUNK
    nb = d_inner // SSD_STATE
    return pl.pallas_call(
        functools.partial(_ssd_kernel, npairs=npairs),
        grid=(SSD_GROUPS, l // ch),
        in_specs=[
            pl.BlockSpec((ch, gw), lambda g, c: (c, g)),
            pl.BlockSpec((ch, SSD_STATE), lambda g, c: (c, nb + g)),
            pl.BlockSpec((ch, SSD_STATE), lambda g, c: (c, nb + SSD_GROUPS + g)),
            pl.BlockSpec((hpg, ch), lambda g, c: (g, c)),
            pl.BlockSpec((hpg, ch), lambda g, c: (g, c)),
            pl.BlockSpec((1, ch, hpg), lambda g, c: (g, c, 0)),
            pl.BlockSpec((ch, gw), lambda g, c: (c, g)),
            pl.BlockSpec((1, gw), lambda g, c: (0, g)),
            pl.BlockSpec((1, gw), lambda g, c: (0, g)),
        ],
        out_specs=pl.BlockSpec((ch, gw), lambda g, c: (c, g)),
        out_shape=jax.ShapeDtypeStruct((l, d_inner), BF16),
        scratch_shapes=[pltpu.VMEM((npairs, SSD_STATE, 2 * SSD_HEAD_DIM), F32),
                        pltpu.VMEM((ch, gw), F32)],
        compiler_params=_params(("arbitrary", "arbitrary")),
        name="ssd_scan",
    )(xbc, xbc, xbc, dtr, acsr, acsc, z, d_exp, norm_w.reshape(1, d_inner))


def _moe_up_kernel(h_ref, wg_ref, wl_ref, bg_ref, bl_ref, comb_ref, o_ref):
    e = pl.program_id(1)
    h = h_ref[...]
    g = jnp.dot(h, wg_ref[0], preferred_element_type=F32) + bg_ref[0]
    lin = jnp.dot(h, wl_ref[0], preferred_element_type=F32) + bl_ref[0]
    xg = jnp.minimum(g, SWIGLU_LIMIT)
    xl = jnp.clip(lin, -SWIGLU_LIMIT, SWIGLU_LIMIT)
    act = xg * _sigmoid(SWIGLU_ALPHA * xg) * (xl + 1.0)
    comb = comb_ref[...]
    lane = lax.broadcasted_iota(jnp.int32, comb.shape, 1)
    ce = jnp.sum(jnp.where(lane == e, comb, 0.0), axis=1, keepdims=True)
    o_ref[...] = (act * ce).astype(o_ref.dtype)


def _moe_up(h, w_g, w_l, b_g, b_l, comb, tm=1024):
    t, d = h.shape
    ne, _, f = w_g.shape
    tm = _tile(t, tm)
    return pl.pallas_call(
        _moe_up_kernel,
        grid=(t // tm, ne),
        in_specs=[
            pl.BlockSpec((tm, d), lambda i, e: (i, 0)),
            pl.BlockSpec((1, d, f), lambda i, e: (e, 0, 0)),
            pl.BlockSpec((1, d, f), lambda i, e: (e, 0, 0)),
            pl.BlockSpec((1, 1, f), lambda i, e: (e, 0, 0)),
            pl.BlockSpec((1, 1, f), lambda i, e: (e, 0, 0)),
            pl.BlockSpec((tm, LANES), lambda i, e: (i, 0)),
        ],
        out_specs=pl.BlockSpec((tm, f), lambda i, e: (i, e)),
        out_shape=jax.ShapeDtypeStruct((t, ne * f), BF16),
        compiler_params=_params(("arbitrary", "arbitrary")),
        name="moe_up",
    )(h, w_g, w_l, b_g, b_l, comb)


def kernel(x, c, w_cond, b_cond, w_mod, b_mod, g_mix_pre, g_mix_post, g_ffn_pre, g_ffn_post, attn_w_in, attn_w_out, ssd_w_in, ssd_conv_w, ssd_conv_b, ssd_dt_bias, ssd_a_log, ssd_d, ssd_norm_w, ssd_w_out, moe_w_router, moe_b_router, moe_w_gu, moe_b_gu, moe_w_down, moe_b_down):
    bsz, seq, d = x.shape
    assert bsz == 1, "kernel is written for a single sequence"
    depth = w_mod.shape[0]
    ne = moe_w_router.shape[-1]
    ff = moe_w_down.shape[2]
    d_inner = ssd_w_out.shape[1]
    conv_dim = ssd_conv_w.shape[-1]

    mod = _conditioning(c, w_cond, b_cond, w_mod, b_mod)

    def mods(i):
        return [mod[i, k * d:(k + 1) * d] for k in range(N_MOD)]

    xt = x.reshape(seq, d)
    sh_m, sc_m, ga_m, sh_f, sc_f, ga_f = mods(0)
    (h,) = _glue(xt, nxt=(g_mix_pre[0], sc_m, sh_m))

    for i in range(depth):
        j = i // 2
        if i % 2 == 0:
            w_in = attn_w_in[j]
            q = _matmul(h, w_in[:, :d].astype(BF16), BF16, scale=SB_HEAD_DIM ** -0.5)
            kv = _matmul(h, w_in[:, d:].astype(BF16), BF16)
            o = _attention(q, kv)
            y = _matmul(o, attn_w_out[j].astype(BF16), F32)
        else:
            w_in = ssd_w_in[j]
            z = _matmul(h, w_in[:, :d_inner].astype(BF16), F32)
            xbc = _matmul(h, w_in[:, d_inner:d_inner + conv_dim].astype(BF16), F32)
            dt_raw = _matmul(h, w_in[:, d_inner + conv_dim:].astype(BF16), F32)
            xbc = _conv_silu(xbc, ssd_conv_w[j], ssd_conv_b[j])
            dtr, acsr, acsc = _dtprep(dt_raw, ssd_dt_bias[j], ssd_a_log[j])
            d_exp = jnp.repeat(ssd_d[j], SSD_HEAD_DIM).reshape(1, d_inner)
            yn = _ssd_scan(xbc, z, dtr, acsr, acsc, d_exp, ssd_norm_w[j])
            y = _matmul(yn, ssd_w_out[j].astype(BF16), F32, tk=2048)

        wr = jnp.pad(moe_w_router[i], ((0, 0), (0, LANES - ne)))
        br = jnp.pad(moe_b_router[i], (0, LANES - ne), constant_values=NEG_INF).reshape(1, LANES)
        xt, h, comb = _glue(xt, resid=(y, ga_m, g_mix_post[i]),
                            nxt=(g_ffn_pre[i], sc_f, sh_f), router=(wr, br))

        w_g = moe_w_gu[i][:, :, 0::2].astype(BF16)
        w_l = moe_w_gu[i][:, :, 1::2].astype(BF16)
        b_g = moe_b_gu[i][:, 0::2].reshape(ne, 1, ff)
        b_l = moe_b_gu[i][:, 1::2].reshape(ne, 1, ff)
        act = _moe_up(h, w_g, w_l, b_g, b_l, comb)
        bd = jnp.pad(moe_b_down[i], ((0, LANES - ne), (0, 0)))
        y = _matmul(act, moe_w_down[i].reshape(ne * ff, d).astype(BF16), F32,
                    bias=(comb, bd), tk=2048)

        ga_prev = ga_f
        if i + 1 < depth:
            sh_m, sc_m, ga_m, sh_f, sc_f, ga_f = mods(i + 1)
            xt, h = _glue(xt, resid=(y, ga_prev, g_ffn_post[i]),
                          nxt=(g_mix_pre[i + 1], sc_m, sh_m))
        else:
            (xt,) = _glue(xt, resid=(y, ga_prev, g_ffn_post[i]))

    return xt.reshape(bsz, seq, d)
```

```python
import functools
import math

import jax
import jax.numpy as jnp
from jax import lax
from jax.experimental import pallas as pl
from jax.experimental.pallas import tpu as pltpu

SB_HEAD_DIM = 128
SSD_HEAD_DIM = 64
SSD_GROUPS = 8
SSD_STATE = 128
SSD_CONV = 4
SSD_CHUNK = 128
TOP_K = 4
SWIGLU_ALPHA = 1.702
SWIGLU_LIMIT = 7.0
N_MOD = 6
NORM_EPS = 1e-6

LANES = 128
SUBLANES = 8
VMEM_LIMIT_BYTES = 56 * 1024 * 1024

F32 = jnp.float32
BF16 = jnp.bfloat16
HIGHEST = lax.Precision.HIGHEST
NEG_INF = float("-inf")
LOG2E = math.log2(math.e)


def _params(sem):
    return pltpu.CompilerParams(dimension_semantics=sem, vmem_limit_bytes=VMEM_LIMIT_BYTES)


def _sigmoid(x):
    return 1.0 / (1.0 + jnp.exp(-x))


def _softplus(x):
    return jnp.maximum(x, 0.0) + jnp.log(1.0 + jnp.exp(-jnp.abs(x)))


def _tile(n, pref):
    if n <= pref:
        return n
    t = pref
    while n % t:
        t //= 2
    return t


def _cond_kernel(c_ref, w_ref, b_ref, o_ref):
    c = c_ref[...]
    s = c * _sigmoid(c)
    r = jnp.dot(s, w_ref[...], precision=HIGHEST, preferred_element_type=F32) + b_ref[...]
    o_ref[...] = r * _sigmoid(r)


def _mod_kernel(cond_ref, w_ref, b_ref, o_ref):
    o_ref[0] = jnp.dot(cond_ref[...], w_ref[0], precision=HIGHEST,
                       preferred_element_type=F32) + b_ref[0]


def _conditioning(c, w_cond, b_cond, w_mod, b_mod):
    bsz, d = c.shape
    rank = w_cond.shape[1]
    depth, _, nmod = w_mod.shape
    rows = SUBLANES
    c8 = jnp.broadcast_to(c[:1], (rows, d))
    cond = pl.pallas_call(
        _cond_kernel,
        out_shape=jax.ShapeDtypeStruct((rows, rank), F32),
        compiler_params=_params(None),
        name="cond_trunk",
    )(c8, w_cond, b_cond.reshape(1, rank))
    tn = _tile(nmod, 4096)
    mod = pl.pallas_call(
        _mod_kernel,
        grid=(depth, nmod // tn),
        in_specs=[
            pl.BlockSpec((rows, rank), lambda i, j: (0, 0)),
            pl.BlockSpec((1, rank, tn), lambda i, j: (i, 0, j)),
            pl.BlockSpec((1, 1, tn), lambda i, j: (i, 0, j)),
        ],
        out_specs=pl.BlockSpec((1, rows, tn), lambda i, j: (i, 0, j)),
        out_shape=jax.ShapeDtypeStruct((depth, rows, nmod), F32),
        compiler_params=_params(("arbitrary", "arbitrary")),
        name="cond_mod",
    )(cond, w_mod, b_mod.reshape(depth, 1, nmod))
    return mod[:, 0, :]


def _rms(y):
    return y * lax.rsqrt(jnp.mean(y * y, axis=-1, keepdims=True) + NORM_EPS)


def _glue_kernel(*refs, has_resid, has_next, has_router):
    it = iter(refs)
    x_ref = next(it)
    if has_resid:
        y_ref, ga_ref, gpost_ref = next(it), next(it), next(it)
    if has_next:
        gpre_ref, sc_ref, sh_ref = next(it), next(it), next(it)
    if has_router:
        wr_ref, br_ref = next(it), next(it)
    if has_resid:
        xo_ref = next(it)
    if has_next:
        h_ref = next(it)
    if has_router:
        comb_ref = next(it)

    x = x_ref[...]
    if has_resid:
        x = x + ga_ref[...] * (_rms(y_ref[...]) * gpost_ref[...])
        xo_ref[...] = x
    if has_next:
        h = (_rms(x) * gpre_ref[...]) * (1.0 + sc_ref[...]) + sh_ref[...]
        h_ref[...] = h.astype(h_ref.dtype)
    if has_router:
        logits = jnp.dot(h, wr_ref[...], precision=HIGHEST,
                         preferred_element_type=F32) + br_ref[...]
        lane = lax.broadcasted_iota(jnp.int32, logits.shape, 1).astype(F32)
        work = logits
        vals, sels = [], []
        for _ in range(TOP_K):
            m = jnp.max(work, axis=1, keepdims=True)
            idx = jnp.min(jnp.where(work == m, lane, float(LANES)), axis=1, keepdims=True)
            sel = lane == idx
            vals.append(m)
            sels.append(sel)
            work = jnp.where(sel, NEG_INF, work)
        exps = [jnp.exp(v - vals[0]) for v in vals]
        denom = exps[0]
        for e in exps[1:]:
            denom = denom + e
        comb = jnp.zeros_like(logits)
        for sel, e in zip(sels, exps):
            comb = jnp.where(sel, e / denom, comb)
        comb_ref[...] = comb


def _glue(x, resid=None, nxt=None, router=None, tm=128):
    t, d = x.shape
    tm = _tile(t, tm)
    row = pl.BlockSpec((tm, d), lambda i: (i, 0))
    vec = pl.BlockSpec((1, d), lambda i: (0, 0))
    args, in_specs, out_shape, out_specs = [x], [row], [], []
    if resid is not None:
        y, ga, gpost = resid
        args += [y, ga.reshape(1, d), gpost.reshape(1, d)]
        in_specs += [row, vec, vec]
        out_shape.append(jax.ShapeDtypeStruct((t, d), F32))
        out_specs.append(row)
    if nxt is not None:
        gpre, sc, sh = nxt
        args += [gpre.reshape(1, d), sc.reshape(1, d), sh.reshape(1, d)]
        in_specs += [vec, vec, vec]
        out_shape.append(jax.ShapeDtypeStruct((t, d), BF16))
        out_specs.append(row)
    if router is not None:
        wr, br = router
        args += [wr, br]
        in_specs += [pl.BlockSpec((d, LANES), lambda i: (0, 0)),
                     pl.BlockSpec((1, LANES), lambda i: (0, 0))]
        out_shape.append(jax.ShapeDtypeStruct((t, LANES), F32))
        out_specs.append(pl.BlockSpec((tm, LANES), lambda i: (i, 0)))
    outs = pl.pallas_call(
        functools.partial(_glue_kernel, has_resid=resid is not None,
                          has_next=nxt is not None, has_router=router is not None),
        grid=(t // tm,),
        in_specs=in_specs,
        out_specs=out_specs,
        out_shape=out_shape,
        compiler_params=_params(("arbitrary",)),
        name="glue",
    )(*args)
    return outs


def _mm_kernel(*refs, nk, scale, has_bias):
    if has_bias:
        a_ref, w_ref, cmb_ref, bd_ref, o_ref = refs[:5]
        rest = refs[5:]
    else:
        a_ref, w_ref, o_ref = refs[:3]
        rest = refs[3:]
    part = jnp.dot(a_ref[...], w_ref[...], preferred_element_type=F32)
    if nk == 1:
        if has_bias:
            part = part + jnp.dot(cmb_ref[...], bd_ref[...], precision=HIGHEST,
                                  preferred_element_type=F32)
        if scale != 1.0:
            part = part * scale
        o_ref[...] = part.astype(o_ref.dtype)
        return
    acc_ref = rest[0] if rest else o_ref
    k = pl.program_id(2)

    @pl.when(k == 0)
    def _():
        init = part
        if has_bias:
            init = init + jnp.dot(cmb_ref[...], bd_ref[...], precision=HIGHEST,
                                  preferred_element_type=F32)
        acc_ref[...] = init

    @pl.when(k > 0)
    def _():
        acc_ref[...] += part

    if rest or scale != 1.0:
        @pl.when(k == nk - 1)
        def _():
            r = acc_ref[...]
            if scale != 1.0:
                r = r * scale
            o_ref[...] = r.astype(o_ref.dtype)


def _matmul(a, w, out_dtype, scale=1.0, bias=None, tm=1024, tn=1024, tk=4096):
    m, k = a.shape
    _, n = w.shape
    tm, tn, tk = _tile(m, tm), _tile(n, tn), _tile(k, tk)
    nk = k // tk
    args = [a, w]
    in_specs = [pl.BlockSpec((tm, tk), lambda i, j, kk: (i, kk)),
                pl.BlockSpec((tk, tn), lambda i, j, kk: (kk, j))]
    if bias is not None:
        args += list(bias)
        in_specs += [pl.BlockSpec((tm, LANES), lambda i, j, kk: (i, 0)),
                     pl.BlockSpec((LANES, tn), lambda i, j, kk: (0, j))]
    scratch = []
    if nk > 1 and out_dtype != F32:
        scratch = [pltpu.VMEM((tm, tn), F32)]
    return pl.pallas_call(
        functools.partial(_mm_kernel, nk=nk, scale=scale, has_bias=bias is not None),
        grid=(m // tm, n // tn, nk),
        in_specs=in_specs,
        out_specs=pl.BlockSpec((tm, tn), lambda i, j, kk: (i, j)),
        out_shape=jax.ShapeDtypeStruct((m, n), out_dtype),
        scratch_shapes=scratch,
        compiler_params=_params(("arbitrary", "arbitrary", "arbitrary")),
        name="matmul",
    )(*args)


def _attn_kernel(q_ref, k_ref, v_ref, o_ref, acc_ref, carry_ref, *, tq, tk):
    i = pl.program_id(1)
    r2 = lax.broadcasted_iota(jnp.int32, (tk, tk), 0)
    c2 = lax.broadcasted_iota(jnp.int32, (tk, tk), 1)
    later = jnp.where(r2 > c2, 1.0, 0.0).astype(BF16)

    acc_ref[...] = jnp.zeros(acc_ref.shape, F32)
    carry_ref[...] = jnp.zeros(carry_ref.shape, F32)

    def span(k_start, nsub, r0, diag):
        kk = k_ref[pl.ds(k_start, nsub * tk), :]
        vv = v_ref[pl.ds(k_start, nsub * tk), :]
        n = lax.dot_general(q_ref[r0:, :], kk, (((1,), (1,)), ((), ())),
                            preferred_element_type=F32)
        neg_abs = lax.bitcast_convert_type(
            lax.bitcast_convert_type(n, jnp.uint32) | jnp.uint32(0x80000000), F32)
        lk = jnp.minimum(n, 0.0) - jnp.log(1.0 + jnp.exp2(neg_abs)) * LOG2E
        if diag:
            row = lax.broadcasted_iota(jnp.int32, n.shape, 0)
            col = lax.broadcasted_iota(jnp.int32, n.shape, 1)
            vis = col < row
            lk = jnp.where(vis, lk, 0.0)
        carry = carry_ref[r0:, :]
        ws = [None] * nsub
        for s in range(nsub - 1, -1, -1):
            lk_s = lk[:, s * tk:(s + 1) * tk]
            la = jnp.dot(lk_s.astype(BF16), later, preferred_element_type=F32) + carry
            ws[s] = jnp.exp2((la + lk_s) - n[:, s * tk:(s + 1) * tk])
            carry = la[:, 0:1] + lk_s[:, 0:1]
        w = ws[0] if nsub == 1 else jnp.concatenate(ws, axis=1)
        if diag:
            w = jnp.where(vis, w, 0.0)
        acc_ref[r0:, :] += jnp.dot(w.astype(BF16), vv, preferred_element_type=F32)
        carry_ref[r0:, :] = carry

    q0 = i * tq
    for d in range(tq // tk - 1, -1, -1):
        span(pl.multiple_of(q0 + d * tk, tk), 1, d * tk, True)

    nsub = math.gcd(tq // tk, 4)
    nstep = q0 // (nsub * tk)

    def body(t, _):
        span(pl.multiple_of((nstep - 1 - t) * (nsub * tk), nsub * tk), nsub, 0, False)
        return 0

    lax.fori_loop(0, nstep, body, 0)
    o_ref[...] = acc_ref[...].astype(o_ref.dtype)


def _attention(q, kv, tq=1024, tk=256):
    l, d = q.shape
    nh = d // SB_HEAD_DIM
    tq = _tile(l, tq)
    tk = _tile(tq, tk)
    return pl.pallas_call(
        functools.partial(_attn_kernel, tq=tq, tk=tk),
        grid=(nh, l // tq),
        in_specs=[
            pl.BlockSpec((tq, SB_HEAD_DIM), lambda h, i: (i, h)),
            pl.BlockSpec((l, SB_HEAD_DIM), lambda h, i: (0, h)),
            pl.BlockSpec((l, SB_HEAD_DIM), lambda h, i: (0, nh + h)),
        ],
        out_specs=pl.BlockSpec((tq, SB_HEAD_DIM), lambda h, i: (i, h)),
        out_shape=jax.ShapeDtypeStruct((l, d), BF16),
        scratch_shapes=[pltpu.VMEM((tq, SB_HEAD_DIM), F32), pltpu.VMEM((tq, 1), F32)],
        compiler_params=_params(("arbitrary", "arbitrary")),
        name="sb_attention",
    )(q, kv, kv)


def _conv_kernel(u_ref, w_ref, b_ref, o_ref, ext_ref, *, tl):
    halo = SUBLANES

    @pl.when(pl.program_id(1) == 0)
    def _():
        ext_ref[0:halo, :] = jnp.zeros((halo, ext_ref.shape[1]), F32)

    u = u_ref[...]
    ext_ref[halo:halo + tl, :] = u
    acc = b_ref[...] + w_ref[SSD_CONV - 1:SSD_CONV, :] * u
    for k in range(SSD_CONV - 1):
        off = halo - (SSD_CONV - 1) + k
        acc = acc + w_ref[k:k + 1, :] * ext_ref[off:off + tl, :]
    o_ref[...] = acc * _sigmoid(acc)
    ext_ref[0:halo, :] = u[tl - halo:tl, :]


def _conv_silu(u, w, b, tl=512, tc=1024):
    l, c = u.shape
    tl, tc = _tile(l, tl), _tile(c, tc)
    return pl.pallas_call(
        functools.partial(_conv_kernel, tl=tl),
        grid=(c // tc, l // tl),
        in_specs=[
            pl.BlockSpec((tl, tc), lambda j, i: (i, j)),
            pl.BlockSpec((SSD_CONV, tc), lambda j, i: (0, j)),
            pl.BlockSpec((1, tc), lambda j, i: (0, j)),
        ],
        out_specs=pl.BlockSpec((tl, tc), lambda j, i: (i, j)),
        out_shape=jax.ShapeDtypeStruct((l, c), F32),
        scratch_shapes=[pltpu.VMEM((tl + SUBLANES, tc), F32)],
        compiler_params=_params(("arbitrary", "arbitrary")),
        name="ssd_conv",
    )(u, w, b.reshape(1, c))


def _split3(x):
    p1 = x.astype(BF16)
    r1 = x - p1.astype(F32)
    p2 = r1.astype(BF16)
    p3 = (r1 - p2.astype(F32)).astype(BF16)
    return p1, p2, p3


def _dtprep_kernel(dt_ref, bias_ref, alog_ref, dtr_ref, acsr_ref, acsc_ref, *, hpg):
    dt = _softplus(dt_ref[...] + bias_ref[...])
    adt = dt * (-jnp.exp(alog_ref[...]))
    n = dt.shape[0]
    r = lax.broadcasted_iota(jnp.int32, (n, n), 0)
    c = lax.broadcasted_iota(jnp.int32, (n, n), 1)
    upto = jnp.where(c <= r, 1.0, 0.0).astype(BF16)
    acs = sum(jnp.dot(upto, p, preferred_element_type=F32) for p in _split3(adt))
    dtr_ref[...] = dt.T
    acsr_ref[...] = acs.T
    for g in range(acsc_ref.shape[0]):
        acsc_ref[g] = acs[:, g * hpg:(g + 1) * hpg]


def _dtprep(dt_raw, dt_bias, a_log):
    l, nh = dt_raw.shape
    hpg = nh // SSD_GROUPS
    ch = SSD_CHUNK
    return pl.pallas_call(
        functools.partial(_dtprep_kernel, hpg=hpg),
        grid=(l // ch,),
        in_specs=[
            pl.BlockSpec((ch, nh), lambda i: (i, 0)),
            pl.BlockSpec((1, nh), lambda i: (0, 0)),
            pl.BlockSpec((1, nh), lambda i: (0, 0)),
        ],
        out_specs=[
            pl.BlockSpec((nh, ch), lambda i: (0, i)),
            pl.BlockSpec((nh, ch), lambda i: (0, i)),
            pl.BlockSpec((SSD_GROUPS, ch, hpg), lambda i: (0, i, 0)),
        ],
        out_shape=[
            jax.ShapeDtypeStruct((nh, l), F32),
            jax.ShapeDtypeStruct((nh, l), F32),
            jax.ShapeDtypeStruct((SSD_GROUPS, l, hpg), F32),
        ],
        compiler_params=_params(("arbitrary",)),
        name="ssd_dtprep",
    )(dt_raw, dt_bias.reshape(1, nh), a_log.reshape(1, nh))


def _ssd_kernel(xs_ref, b_ref, c_ref, dtr_ref, acsr_ref, acsc_ref, z_ref, dexp_ref, nw_ref,
                o_ref, st_ref, y_ref, *, npairs):
    ch = SSD_CHUNK
    hp = SSD_HEAD_DIM

    @pl.when(pl.program_id(1) == 0)
    def _():
        st_ref[...] = jnp.zeros(st_ref.shape, F32)

    bm = b_ref[...]
    cm = c_ref[...]
    cb = lax.dot_general(cm.astype(BF16), bm.astype(BF16), (((1,), (1,)), ((), ())),
                         preferred_element_type=F32)
    bt = bm.T
    row = lax.broadcasted_iota(jnp.int32, (ch, ch), 0)
    col = lax.broadcasted_iota(jnp.int32, (ch, ch), 1)
    causal = col <= row
    first = lax.broadcasted_iota(jnp.int32, (ch, 2 * hp), 1) < hp
    dtr = dtr_ref[...]
    acsr = acsr_ref[...]
    acsc = acsc_ref[0]

    for pr in range(npairs):
        sl = slice(pr * 2 * hp, (pr + 1) * 2 * hp)
        xs_b = xs_ref[:, sl].astype(BF16)
        prev = st_ref[pr]
        rhs = jnp.concatenate([xs_b, prev.astype(BF16)], axis=0)
        outs, news = [], []
        for hh in (2 * pr, 2 * pr + 1):
            a_col = jnp.broadcast_to(acsc[:, hh:hh + 1], (ch, ch))
            a_row = acsr[hh:hh + 1, :]
            dt_row = dtr[hh:hh + 1, :]
            lmat = jnp.exp(jnp.where(causal, a_col - a_row, NEG_INF))
            scores = cb * lmat * dt_row
            c_dec = cm * jnp.exp(a_col)
            lhs = jnp.concatenate([scores.astype(BF16), c_dec.astype(BF16)], axis=1)
            outs.append(jnp.dot(lhs, rhs, preferred_element_type=F32))
            a_last = a_row[:, ch - 1:ch]
            b_dec = (bt * (dt_row * jnp.exp(a_last - a_row))).astype(BF16)
            news.append(jnp.exp(a_last) * prev
                        + jnp.dot(b_dec, xs_b, preferred_element_type=F32))
        y_ref[:, sl] = jnp.where(first, outs[0], outs[1])
        st_ref[pr] = jnp.where(first, news[0], news[1])

    y = y_ref[...] + dexp_ref[...] * xs_ref[...]
    z = z_ref[...]
    yg = y * (z * _sigmoid(z))
    o_ref[...] = (_rms(yg) * nw_ref[...]).astype(o_ref.dtype)


def _ssd_scan(xbc, z, dtr, acsr, acsc, d_exp, norm_w):
    l, d_inner = z.shape
    gw = d_inner // SSD_GROUPS
    npairs = gw // (2 * SSD_HEAD_DIM)
    hpg = gw // SSD_HEAD_DIM
    ch = SSD_CHUNK
    nb = d_inner // SSD_STATE
    return pl.pallas_call(
        functools.partial(_ssd_kernel, npairs=npairs),
        grid=(SSD_GROUPS, l // ch),
        in_specs=[
            pl.BlockSpec((ch, gw), lambda g, c: (c, g)),
            pl.BlockSpec((ch, SSD_STATE), lambda g, c: (c, nb + g)),
            pl.BlockSpec((ch, SSD_STATE), lambda g, c: (c, nb + SSD_GROUPS + g)),
            pl.BlockSpec((hpg, ch), lambda g, c: (g, c)),
            pl.BlockSpec((hpg, ch), lambda g, c: (g, c)),
            pl.BlockSpec((1, ch, hpg), lambda g, c: (g, c, 0)),
            pl.BlockSpec((ch, gw), lambda g, c: (c, g)),
            pl.BlockSpec((1, gw), lambda g, c: (0, g)),
            pl.BlockSpec((1, gw), lambda g, c: (0, g)),
        ],
        out_specs=pl.BlockSpec((ch, gw), lambda g, c: (c, g)),
        out_shape=jax.ShapeDtypeStruct((l, d_inner), BF16),
        scratch_shapes=[pltpu.VMEM((npairs, SSD_STATE, 2 * SSD_HEAD_DIM), F32),
                        pltpu.VMEM((ch, gw), F32)],
        compiler_params=_params(("arbitrary", "arbitrary")),
        name="ssd_scan",
    )(xbc, xbc, xbc, dtr, acsr, acsc, z, d_exp, norm_w.reshape(1, d_inner))


def _deinterleave_kernel(w_ref, p_ref, o_ref):
    o_ref[0] = jnp.dot(w_ref[0].astype(BF16), p_ref[...],
                       preferred_element_type=F32).astype(BF16)


def _deinterleave_cast(w_gu, tr=1024):
    ne, d, f2 = w_gu.shape
    tr = _tile(d, tr)
    src = jnp.arange(f2)
    dst = jnp.where(src % 2 == 0, src // 2, f2 // 2 + src // 2)
    perm = (dst[:, None] == jnp.arange(f2)[None, :]).astype(BF16)
    return pl.pallas_call(
        _deinterleave_kernel,
        grid=(ne, d // tr),
        in_specs=[pl.BlockSpec((1, tr, f2), lambda e, i: (e, i, 0)),
                  pl.BlockSpec((f2, f2), lambda e, i: (0, 0))],
        out_specs=pl.BlockSpec((1, tr, f2), lambda e, i: (e, i, 0)),
        out_shape=jax.ShapeDtypeStruct((ne, d, f2), BF16),
        compiler_params=_params(("arbitrary", "arbitrary")),
        name="moe_deinterleave",
    )(w_gu, perm)


def _moe_up_kernel(h_ref, w_ref, bg_ref, bl_ref, comb_ref, o_ref):
    e = pl.program_id(1)
    f = o_ref.shape[1]
    gl = jnp.dot(h_ref[...], w_ref[0], preferred_element_type=F32)
    g = gl[:, :f] + bg_ref[0]
    lin = gl[:, f:] + bl_ref[0]
    xg = jnp.minimum(g, SWIGLU_LIMIT)
    xl = jnp.clip(lin, -SWIGLU_LIMIT, SWIGLU_LIMIT)
    act = xg * _sigmoid(SWIGLU_ALPHA * xg) * (xl + 1.0)
    comb = comb_ref[...]
    lane = lax.broadcasted_iota(jnp.int32, comb.shape, 1)
    ce = jnp.sum(jnp.where(lane == e, comb, 0.0), axis=1, keepdims=True)
    o_ref[...] = (act * ce).astype(o_ref.dtype)


def _moe_up(h, w_gl, b_g, b_l, comb, tm=1024):
    t, d = h.shape
    ne, _, f2 = w_gl.shape
    f = f2 // 2
    tm = _tile(t, tm)
    return pl.pallas_call(
        _moe_up_kernel,
        grid=(t // tm, ne),
        in_specs=[
            pl.BlockSpec((tm, d), lambda i, e: (i, 0)),
            pl.BlockSpec((1, d, f2), lambda i, e: (e, 0, 0)),
            pl.BlockSpec((1, 1, f), lambda i, e: (e, 0, 0)),
            pl.BlockSpec((1, 1, f), lambda i, e: (e, 0, 0)),
            pl.BlockSpec((tm, LANES), lambda i, e: (i, 0)),
        ],
        out_specs=pl.BlockSpec((tm, f), lambda i, e: (i, e)),
        out_shape=jax.ShapeDtypeStruct((t, ne * f), BF16),
        compiler_params=_params(("arbitrary", "arbitrary")),
        name="moe_up",
    )(h, w_gl, b_g, b_l, comb)


def kernel(x, c, w_cond, b_cond, w_mod, b_mod, g_mix_pre, g_mix_post, g_ffn_pre, g_ffn_post, attn_w_in, attn_w_out, ssd_w_in, ssd_conv_w, ssd_conv_b, ssd_dt_bias, ssd_a_log, ssd_d, ssd_norm_w, ssd_w_out, moe_w_router, moe_b_router, moe_w_gu, moe_b_gu, moe_w_down, moe_b_down):
    bsz, seq, d = x.shape
    assert bsz == 1, "kernel is written for a single sequence"
    depth = w_mod.shape[0]
    ne = moe_w_router.shape[-1]
    ff = moe_w_down.shape[2]
    d_inner = ssd_w_out.shape[1]
    conv_dim = ssd_conv_w.shape[-1]

    mod = _conditioning(c, w_cond, b_cond, w_mod, b_mod)

    def mods(i):
        return [mod[i, k * d:(k + 1) * d] for k in range(N_MOD)]

    xt = x.reshape(seq, d)
    sh_m, sc_m, ga_m, sh_f, sc_f, ga_f = mods(0)
    (h,) = _glue(xt, nxt=(g_mix_pre[0], sc_m, sh_m))

    for i in range(depth):
        j = i // 2
        if i % 2 == 0:
            w_in = attn_w_in[j]
            q = _matmul(h, w_in[:, :d].astype(BF16), BF16, scale=-LOG2E * SB_HEAD_DIM ** -0.5)
            kv = _matmul(h, w_in[:, d:].astype(BF16), BF16)
            o = _attention(q, kv)
            y = _matmul(o, attn_w_out[j].astype(BF16), F32)
        else:
            w_in = ssd_w_in[j]
            z = _matmul(h, w_in[:, :d_inner].astype(BF16), F32)
            xbc = _matmul(h, w_in[:, d_inner:d_inner + conv_dim].astype(BF16), F32)
            dt_raw = _matmul(h, w_in[:, d_inner + conv_dim:].astype(BF16), F32)
            xbc = _conv_silu(xbc, ssd_conv_w[j], ssd_conv_b[j])
            dtr, acsr, acsc = _dtprep(dt_raw, ssd_dt_bias[j], ssd_a_log[j])
            d_exp = jnp.repeat(ssd_d[j], SSD_HEAD_DIM).reshape(1, d_inner)
            yn = _ssd_scan(xbc, z, dtr, acsr, acsc, d_exp, ssd_norm_w[j])
            y = _matmul(yn, ssd_w_out[j].astype(BF16), F32)

        wr = jnp.pad(moe_w_router[i], ((0, 0), (0, LANES - ne)))
        br = jnp.pad(moe_b_router[i], (0, LANES - ne), constant_values=NEG_INF).reshape(1, LANES)
        xt, h, comb = _glue(xt, resid=(y, ga_m, g_mix_post[i]),
                            nxt=(g_ffn_pre[i], sc_f, sh_f), router=(wr, br))

        w_gl = _deinterleave_cast(moe_w_gu[i])
        b_g = moe_b_gu[i][:, 0::2].reshape(ne, 1, ff)
        b_l = moe_b_gu[i][:, 1::2].reshape(ne, 1, ff)
        act = _moe_up(h, w_gl, b_g, b_l, comb)
        bd = jnp.pad(moe_b_down[i], ((0, LANES - ne), (0, 0)))
        y = _matmul(act, moe_w_down[i].reshape(ne * ff, d).astype(BF16), F32,
                    bias=(comb, bd))

        ga_prev = ga_f
        if i + 1 < depth:
            sh_m, sc_m, ga_m, sh_f, sc_f, ga_f = mods(i + 1)
            xt, h = _glue(xt, resid=(y, ga_prev, g_ffn_post[i]),
                          nxt=(g_mix_pre[i + 1], sc_m, sh_m))
        else:
            (xt,) = _glue(xt, resid=(y, ga_prev, g_ffn_post[i]))

    return xt.reshape(bsz, seq, d)
```

```python
import functools
import math

import jax
import jax.numpy as jnp
from jax import lax
from jax.experimental import pallas as pl
from jax.experimental.pallas import tpu as pltpu

SB_HEAD_DIM = 128
SSD_HEAD_DIM = 64
SSD_GROUPS = 8
SSD_STATE = 128
SSD_CONV = 4
SSD_CHUNK = 128
TOP_K = 4
SWIGLU_ALPHA = 1.702
SWIGLU_LIMIT = 7.0
N_MOD = 6
NORM_EPS = 1e-6

LANES = 128
SUBLANES = 8
VMEM_LIMIT_BYTES = 56 * 1024 * 1024

GLUE_ROWS = 128
MOE_TILE_ROWS = 256
GATHER_UNROLL = 8

F32 = jnp.float32
BF16 = jnp.bfloat16
HIGHEST = lax.Precision.HIGHEST
NEG_INF = float("-inf")
LOG2E = math.log2(math.e)


def _params(sem):
    return pltpu.CompilerParams(dimension_semantics=sem, vmem_limit_bytes=VMEM_LIMIT_BYTES)


def _sigmoid(x):
    return 1.0 / (1.0 + jnp.exp(-x))


def _softplus(x):
    return jnp.maximum(x, 0.0) + jnp.log(1.0 + jnp.exp(-jnp.abs(x)))


def _tile(n, pref):
    if n <= pref:
        return n
    t = pref
    while n % t:
        t //= 2
    return t


def _cond_kernel(c_ref, w_ref, b_ref, o_ref):
    c = c_ref[...]
    s = c * _sigmoid(c)
    r = jnp.dot(s, w_ref[...], precision=HIGHEST, preferred_element_type=F32) + b_ref[...]
    o_ref[...] = r * _sigmoid(r)


def _mod_kernel(cond_ref, w_ref, b_ref, o_ref):
    o_ref[0] = jnp.dot(cond_ref[...], w_ref[0], precision=HIGHEST,
                       preferred_element_type=F32) + b_ref[0]


def _conditioning(c, w_cond, b_cond, w_mod, b_mod):
    bsz, d = c.shape
    rank = w_cond.shape[1]
    depth, _, nmod = w_mod.shape
    rows = SUBLANES
    c8 = jnp.broadcast_to(c[:1], (rows, d))
    cond = pl.pallas_call(
        _cond_kernel,
        out_shape=jax.ShapeDtypeStruct((rows, rank), F32),
        compiler_params=_params(None),
        name="cond_trunk",
    )(c8, w_cond, b_cond.reshape(1, rank))
    tn = _tile(nmod, 4096)
    mod = pl.pallas_call(
        _mod_kernel,
        grid=(depth, nmod // tn),
        in_specs=[
            pl.BlockSpec((rows, rank), lambda i, j: (0, 0)),
            pl.BlockSpec((1, rank, tn), lambda i, j: (i, 0, j)),
            pl.BlockSpec((1, 1, tn), lambda i, j: (i, 0, j)),
        ],
        out_specs=pl.BlockSpec((1, rows, tn), lambda i, j: (i, 0, j)),
        out_shape=jax.ShapeDtypeStruct((depth, rows, nmod), F32),
        compiler_params=_params(("arbitrary", "arbitrary")),
        name="cond_mod",
    )(cond, w_mod, b_mod.reshape(depth, 1, nmod))
    return mod[:, 0, :]


def _rms(y):
    return y * lax.rsqrt(jnp.mean(y * y, axis=-1, keepdims=True) + NORM_EPS)


def _row_copy(src_hbm, row, dst, r, sem):
    return pltpu.make_async_copy(src_hbm.at[pl.ds(row, 1)], dst.at[pl.ds(r, 1)], sem)


def _gather_start(src_hbm, idx_ref, base, n, dst, sem):
    def body(r, carry):
        _row_copy(src_hbm, idx_ref[base + r], dst, r, sem).start()
        return carry

    lax.fori_loop(0, n, body, 0, unroll=GATHER_UNROLL)


def _gather_wait(src_hbm, n, dst, sem):
    def body(r, carry):
        _row_copy(src_hbm, 0, dst, r, sem).wait()
        return carry

    lax.fori_loop(0, n, body, 0, unroll=GATHER_UNROLL)


def _glue_kernel(*refs, has_resid, has_next, has_router, gathered):
    it = iter(refs)
    if gathered:
        pos_ref = next(it)
    x_ref = next(it)
    if has_resid:
        y_ref, ga_ref, gpost_ref = next(it), next(it), next(it)
    if has_next:
        gpre_ref, sc_ref, sh_ref = next(it), next(it), next(it)
    if has_router:
        wr_ref, br_ref = next(it), next(it)
    if has_resid:
        xo_ref = next(it)
    if has_next:
        h_ref = next(it)
    if has_router:
        route_ref = next(it)
    if gathered:
        buf, sem = next(it), next(it)

    x = x_ref[...]
    if has_resid:
        if gathered:
            i = pl.program_id(0)
            tm = x.shape[0]
            nrow = TOP_K * tm
            slot = i % 2

            @pl.when(i == 0)
            def _():
                _gather_start(y_ref, pos_ref, 0, nrow, buf.at[0], sem.at[0])

            @pl.when(i + 1 < pl.num_programs(0))
            def _():
                _gather_start(y_ref, pos_ref, (i + 1) * nrow, nrow, buf.at[1 - slot],
                              sem.at[1 - slot])

            _gather_wait(y_ref, nrow, buf.at[slot], sem.at[slot])
            y = buf[slot, 0:tm, :]
            for k in range(1, TOP_K):
                y = y + buf[slot, k * tm:(k + 1) * tm, :]
        else:
            y = y_ref[...]
        x = x + ga_ref[...] * (_rms(y) * gpost_ref[...])
        xo_ref[...] = x
    if has_next:
        h = (_rms(x) * gpre_ref[...]) * (1.0 + sc_ref[...]) + sh_ref[...]
        h_ref[...] = h.astype(h_ref.dtype)
    if has_router:
        logits = jnp.dot(h, wr_ref[...], precision=HIGHEST,
                         preferred_element_type=F32) + br_ref[...]
        lane = lax.broadcasted_iota(jnp.int32, logits.shape, 1).astype(F32)
        work = logits
        vals, idxs = [], []
        for _ in range(TOP_K):
            m = jnp.max(work, axis=1, keepdims=True)
            idx = jnp.min(jnp.where(work == m, lane, float(LANES)), axis=1, keepdims=True)
            vals.append(m)
            idxs.append(idx)
            work = jnp.where(lane == idx, NEG_INF, work)
        exps = [jnp.exp(v - vals[0]) for v in vals]
        denom = exps[0]
        for e in exps[1:]:
            denom = denom + e
        route = jnp.zeros_like(logits)
        for j in range(TOP_K):
            route = jnp.where(lane == float(j), idxs[j], route)
            route = jnp.where(lane == float(TOP_K + j), exps[j] / denom, route)
        route_ref[...] = route


def _glue(x, resid=None, nxt=None, router=None, gather_pos=None, h_dtype=BF16, tm=GLUE_ROWS):
    t, d = x.shape
    tm = _tile(t, tm)
    gathered = gather_pos is not None
    row = pl.BlockSpec((tm, d), lambda i, *_: (i, 0))
    vec = pl.BlockSpec((1, d), lambda i, *_: (0, 0))
    args, in_specs, out_shape, out_specs, scratch = [x], [row], [], [], []
    if resid is not None:
        y, ga, gpost = resid
        args += [y, ga.reshape(1, d), gpost.reshape(1, d)]
        in_specs += [pl.BlockSpec(memory_space=pl.ANY) if gathered else row, vec, vec]
        out_shape.append(jax.ShapeDtypeStruct((t, d), F32))
        out_specs.append(row)
    if nxt is not None:
        gpre, sc, sh = nxt
        args += [gpre.reshape(1, d), sc.reshape(1, d), sh.reshape(1, d)]
        in_specs += [vec, vec, vec]
        out_shape.append(jax.ShapeDtypeStruct((t, d), h_dtype))
        out_specs.append(row)
    if router is not None:
        wr, br = router
        args += [wr, br]
        in_specs += [pl.BlockSpec((d, LANES), lambda i, *_: (0, 0)),
                     pl.BlockSpec((1, LANES), lambda i, *_: (0, 0))]
        out_shape.append(jax.ShapeDtypeStruct((t, LANES), F32))
        out_specs.append(pl.BlockSpec((tm, LANES), lambda i, *_: (i, 0)))
    if gathered:
        args = [gather_pos] + args
        scratch = [pltpu.VMEM((2, TOP_K * tm, d), F32), pltpu.SemaphoreType.DMA((2,))]
    outs = pl.pallas_call(
        functools.partial(_glue_kernel, has_resid=resid is not None, has_next=nxt is not None,
                          has_router=router is not None, gathered=gathered),
        grid_spec=pltpu.PrefetchScalarGridSpec(
            num_scalar_prefetch=1 if gathered else 0,
            grid=(t // tm,),
            in_specs=in_specs,
            out_specs=out_specs,
            scratch_shapes=scratch,
        ),
        out_shape=out_shape,
        compiler_params=_params(("arbitrary",)),
        name="glue_gather" if gathered else "glue",
    )(*args)
    return outs


def _mm_kernel(a_ref, w_ref, o_ref, *scratch, nk, scale):
    part = jnp.dot(a_ref[...], w_ref[0].astype(BF16), preferred_element_type=F32)
    if nk == 1:
        if scale != 1.0:
            part = part * scale
        o_ref[...] = part.astype(o_ref.dtype)
        return
    acc_ref = scratch[0] if scratch else o_ref
    k = pl.program_id(2)

    @pl.when(k == 0)
    def _():
        acc_ref[...] = part

    @pl.when(k > 0)
    def _():
        acc_ref[...] += part

    if scratch or scale != 1.0:
        @pl.when(k == nk - 1)
        def _():
            r = acc_ref[...]
            if scale != 1.0:
                r = r * scale
            o_ref[...] = r.astype(o_ref.dtype)


def _matmul(a, w, layer, out_dtype, col0=0, ncols=None, scale=1.0, tm=1024, tn=512, tk=4096):
    m, k = a.shape
    n = w.shape[2] - col0 if ncols is None else ncols
    tm, tn, tk = _tile(m, tm), _tile(n, tn), _tile(k, tk)
    assert col0 % tn == 0 and n % tn == 0
    nk = k // tk
    jb = col0 // tn
    scratch = []
    if nk > 1 and out_dtype != F32:
        scratch = [pltpu.VMEM((tm, tn), F32)]
    return pl.pallas_call(
        functools.partial(_mm_kernel, nk=nk, scale=scale),
        grid=(m // tm, n // tn, nk),
        in_specs=[pl.BlockSpec((tm, tk), lambda i, j, kk: (i, kk)),
                  pl.BlockSpec((1, tk, tn), lambda i, j, kk: (layer, kk, jb + j))],
        out_specs=pl.BlockSpec((tm, tn), lambda i, j, kk: (i, j)),
        out_shape=jax.ShapeDtypeStruct((m, n), out_dtype),
        scratch_shapes=scratch,
        compiler_params=_params(("arbitrary", "arbitrary", "arbitrary")),
        name="matmul",
    )(a, w)


def _attn_kernel(q_ref, k_ref, v_ref, o_ref, acc_ref, carry_ref, *, tq, tk):
    i = pl.program_id(1)
    r2 = lax.broadcasted_iota(jnp.int32, (tk, tk), 0)
    c2 = lax.broadcasted_iota(jnp.int32, (tk, tk), 1)
    later = jnp.where(r2 > c2, 1.0, 0.0).astype(BF16)

    acc_ref[...] = jnp.zeros(acc_ref.shape, F32)
    carry_ref[...] = jnp.zeros(carry_ref.shape, F32)

    def span(k_start, nsub, r0, diag):
        kk = k_ref[pl.ds(k_start, nsub * tk), :]
        vv = v_ref[pl.ds(k_start, nsub * tk), :]
        n = lax.dot_general(q_ref[r0:, :], kk, (((1,), (1,)), ((), ())),
                            preferred_element_type=F32)
        neg_abs = lax.bitcast_convert_type(
            lax.bitcast_convert_type(n, jnp.uint32) | jnp.uint32(0x80000000), F32)
        lk = jnp.minimum(n, 0.0) - jnp.log(1.0 + jnp.exp2(neg_abs)) * LOG2E
        if diag:
            row = lax.broadcasted_iota(jnp.int32, n.shape, 0)
            col = lax.broadcasted_iota(jnp.int32, n.shape, 1)
            vis = col < row
            lk = jnp.where(vis, lk, 0.0)
        carry = carry_ref[r0:, :]
        ws = [None] * nsub
        for s in range(nsub - 1, -1, -1):
            lk_s = lk[:, s * tk:(s + 1) * tk]
            la = jnp.dot(lk_s.astype(BF16), later, preferred_element_type=F32) + carry
            ws[s] = jnp.exp2((la + lk_s) - n[:, s * tk:(s + 1) * tk])
            carry = la[:, 0:1] + lk_s[:, 0:1]
        w = ws[0] if nsub == 1 else jnp.concatenate(ws, axis=1)
        if diag:
            w = jnp.where(vis, w, 0.0)
        acc_ref[r0:, :] += jnp.dot(w.astype(BF16), vv, preferred_element_type=F32)
        carry_ref[r0:, :] = carry

    q0 = i * tq
    for d in range(tq // tk - 1, -1, -1):
        span(pl.multiple_of(q0 + d * tk, tk), 1, d * tk, True)

    nsub = math.gcd(tq // tk, 4)
    nstep = q0 // (nsub * tk)

    def body(t, _):
        span(pl.multiple_of((nstep - 1 - t) * (nsub * tk), nsub * tk), nsub, 0, False)
        return 0

    lax.fori_loop(0, nstep, body, 0)
    o_ref[...] = acc_ref[...].astype(o_ref.dtype)


def _attention(q, kv, tq=1024, tk=256):
    l, d = q.shape
    nh = d // SB_HEAD_DIM
    tq = _tile(l, tq)
    tk = _tile(tq, tk)
    return pl.pallas_call(
        functools.partial(_attn_kernel, tq=tq, tk=tk),
        grid=(nh, l // tq),
        in_specs=[
            pl.BlockSpec((tq, SB_HEAD_DIM), lambda h, i: (i, h)),
            pl.BlockSpec((l, SB_HEAD_DIM), lambda h, i: (0, h)),
            pl.BlockSpec((l, SB_HEAD_DIM), lambda h, i: (0, nh + h)),
        ],
        out_specs=pl.BlockSpec((tq, SB_HEAD_DIM), lambda h, i: (i, h)),
        out_shape=jax.ShapeDtypeStruct((l, d), BF16),
        scratch_shapes=[pltpu.VMEM((tq, SB_HEAD_DIM), F32), pltpu.VMEM((tq, 1), F32)],
        compiler_params=_params(("arbitrary", "arbitrary")),
        name="sb_attention",
    )(q, kv, kv)


def _conv_kernel(u_ref, w_ref, b_ref, o_ref, ext_ref, *, tl):
    halo = SUBLANES

    @pl.when(pl.program_id(1) == 0)
    def _():
        ext_ref[0:halo, :] = jnp.zeros((halo, ext_ref.shape[1]), F32)

    u = u_ref[...]
    ext_ref[halo:halo + tl, :] = u
    acc = b_ref[...] + w_ref[SSD_CONV - 1:SSD_CONV, :] * u
    for k in range(SSD_CONV - 1):
        off = halo - (SSD_CONV - 1) + k
        acc = acc + w_ref[k:k + 1, :] * ext_ref[off:off + tl, :]
    o_ref[...] = acc * _sigmoid(acc)
    ext_ref[0:halo, :] = u[tl - halo:tl, :]


def _conv_silu(u, w, b, tl=512, tc=1024):
    l, c = u.shape
    tl, tc = _tile(l, tl), _tile(c, tc)
    return pl.pallas_call(
        functools.partial(_conv_kernel, tl=tl),
        grid=(c // tc, l // tl),
        in_specs=[
            pl.BlockSpec((tl, tc), lambda j, i: (i, j)),
            pl.BlockSpec((SSD_CONV, tc), lambda j, i: (0, j)),
            pl.BlockSpec((1, tc), lambda j, i: (0, j)),
        ],
        out_specs=pl.BlockSpec((tl, tc), lambda j, i: (i, j)),
        out_shape=jax.ShapeDtypeStruct((l, c), F32),
        scratch_shapes=[pltpu.VMEM((tl + SUBLANES, tc), F32)],
        compiler_params=_params(("arbitrary", "arbitrary")),
        name="ssd_conv",
    )(u, w, b.reshape(1, c))


def _split3(x):
    p1 = x.astype(BF16)
    r1 = x - p1.astype(F32)
    p2 = r1.astype(BF16)
    p3 = (r1 - p2.astype(F32)).astype(BF16)
    return p1, p2, p3


def _dtprep_kernel(dt_ref, bias_ref, alog_ref, dtr_ref, acsr_ref, acsc_ref, *, hpg):
    dt = _softplus(dt_ref[...] + bias_ref[...])
    adt = dt * (-jnp.exp(alog_ref[...]))
    n = dt.shape[0]
    r = lax.broadcasted_iota(jnp.int32, (n, n), 0)
    c = lax.broadcasted_iota(jnp.int32, (n, n), 1)
    upto = jnp.where(c <= r, 1.0, 0.0).astype(BF16)
    acs = sum(jnp.dot(upto, p, preferred_element_type=F32) for p in _split3(adt))
    dtr_ref[...] = dt.T
    acsr_ref[...] = acs.T
    for g in range(acsc_ref.shape[0]):
        acsc_ref[g] = acs[:, g * hpg:(g + 1) * hpg]


def _dtprep(dt_raw, dt_bias, a_log):
    l, nh = dt_raw.shape
    hpg = nh // SSD_GROUPS
    ch = SSD_CHUNK
    return pl.pallas_call(
        functools.partial(_dtprep_kernel, hpg=hpg),
        grid=(l // ch,),
        in_specs=[
            pl.BlockSpec((ch, nh), lambda i: (i, 0)),
            pl.BlockSpec((1, nh), lambda i: (0, 0)),
            pl.BlockSpec((1, nh), lambda i: (0, 0)),
        ],
        out_specs=[
            pl.BlockSpec((nh, ch), lambda i: (0, i)),
            pl.BlockSpec((nh, ch), lambda i: (0, i)),
            pl.BlockSpec((SSD_GROUPS, ch, hpg), lambda i: (0, i, 0)),
        ],
        out_shape=[
            jax.ShapeDtypeStruct((nh, l), F32),
            jax.ShapeDtypeStruct((nh, l), F32),
            jax.ShapeDtypeStruct((SSD_GROUPS, l, hpg), F32),
        ],
        compiler_params=_params(("arbitrary",)),
        name="ssd_dtprep",
    )(dt_raw, dt_bias.reshape(1, nh), a_log.reshape(1, nh))


def _ssd_kernel(xs_ref, b_ref, c_ref, dtr_ref, acsr_ref, acsc_ref, z_ref, dexp_ref, nw_ref,
                o_ref, st_ref, y_ref, *, npairs):
    ch = SSD_CHUNK
    hp = SSD_HEAD_DIM

    @pl.when(pl.program_id(1) == 0)
    def _():
        st_ref[...] = jnp.zeros(st_ref.shape, F32)

    bm = b_ref[...]
    cm = c_ref[...]
    cb = lax.dot_general(cm.astype(BF16), bm.astype(BF16), (((1,), (1,)), ((), ())),
                         preferred_element_type=F32)
    bt = bm.T
    row = lax.broadcasted_iota(jnp.int32, (ch, ch), 0)
    col = lax.broadcasted_iota(jnp.int32, (ch, ch), 1)
    causal = col <= row
    first = lax.broadcasted_iota(jnp.int32, (ch, 2 * hp), 1) < hp
    dtr = dtr_ref[...]
    acsr = acsr_ref[...]
    acsc = acsc_ref[0]

    for pr in range(npairs):
        sl = slice(pr * 2 * hp, (pr + 1) * 2 * hp)
        xs_b = xs_ref[:, sl].astype(BF16)
        prev = st_ref[pr]
        rhs = jnp.concatenate([xs_b, prev.astype(BF16)], axis=0)
        outs, news = [], []
        for hh in (2 * pr, 2 * pr + 1):
            a_col = jnp.broadcast_to(acsc[:, hh:hh + 1], (ch, ch))
            a_row = acsr[hh:hh + 1, :]
            dt_row = dtr[hh:hh + 1, :]
            lmat = jnp.exp(jnp.where(causal, a_col - a_row, NEG_INF))
            scores = cb * lmat * dt_row
            c_dec = cm * jnp.exp(a_col)
            lhs = jnp.concatenate([scores.astype(BF16), c_dec.astype(BF16)], axis=1)
            outs.append(jnp.dot(lhs, rhs, preferred_element_type=F32))
            a_last = a_row[:, ch - 1:ch]
            b_dec = (bt * (dt_row * jnp.exp(a_last - a_row))).astype(BF16)
            news.append(jnp.exp(a_last) * prev
                        + jnp.dot(b_dec, xs_b, preferred_element_type=F32))
        y_ref[:, sl] = jnp.where(first, outs[0], outs[1])
        st_ref[pr] = jnp.where(first, news[0], news[1])

    y = y_ref[...] + dexp_ref[...] * xs_ref[...]
    z = z_ref[...]
    yg = y * (z * _sigmoid(z))
    o_ref[...] = (_rms(yg) * nw_ref[...]).astype(o_ref.dtype)


def _ssd_scan(xbc, z, dtr, acsr, acsc, d_exp, norm_w):
    l, d_inner = z.shape
    gw = d_inner // SSD_GROUPS
    npairs = gw // (2 * SSD_HEAD_DIM)
    hpg = gw // SSD_HEAD_DIM
    ch = SSD_CHUNK
    nb = d_inner // SSD_STATE
    return pl.pallas_call(
        functools.partial(_ssd_kernel, npairs=npairs),
        grid=(SSD_GROUPS, l // ch),
        in_specs=[
            pl.BlockSpec((ch, gw), lambda g, c: (c, g)),
            pl.BlockSpec((ch, SSD_STATE), lambda g, c: (c, nb + g)),
            pl.BlockSpec((ch, SSD_STATE), lambda g, c: (c, nb + SSD_GROUPS + g)),
            pl.BlockSpec((hpg, ch), lambda g, c: (g, c)),
            pl.BlockSpec((hpg, ch), lambda g, c: (g, c)),
            pl.BlockSpec((1, ch, hpg), lambda g, c: (g, c, 0)),
            pl.BlockSpec((ch, gw), lambda g, c: (c, g)),
            pl.BlockSpec((1, gw), lambda g, c: (0, g)),
            pl.BlockSpec((1, gw), lambda g, c: (0, g)),
        ],
        out_specs=pl.BlockSpec((ch, gw), lambda g, c: (c, g)),
        out_shape=jax.ShapeDtypeStruct((l, d_inner), BF16),
        scratch_shapes=[pltpu.VMEM((npairs, SSD_STATE, 2 * SSD_HEAD_DIM), F32),
                        pltpu.VMEM((ch, gw), F32)],
        compiler_params=_params(("arbitrary", "arbitrary")),
        name="ssd_scan",
    )(xbc, xbc, xbc, dtr, acsr, acsc, z, d_exp, norm_w.reshape(1, d_inner))


def _deinterleave_kernel(w_ref, p_ref, o_ref):
    o_ref[0] = jnp.dot(w_ref[0, 0].astype(BF16), p_ref[...],
                       preferred_element_type=F32).astype(BF16)


def _deinterleave_cast(w_gu, layer, tr=1024):
    _, ne, d, f2 = w_gu.shape
    tr = _tile(d, tr)
    src = jnp.arange(f2)
    dst = jnp.where(src % 2 == 0, src // 2, f2 // 2 + src // 2)
    perm = (dst[:, None] == jnp.arange(f2)[None, :]).astype(BF16)
    return pl.pallas_call(
        _deinterleave_kernel,
        grid=(ne, d // tr),
        in_specs=[pl.BlockSpec((1, 1, tr, f2), lambda e, i: (layer, e, i, 0)),
                  pl.BlockSpec((f2, f2), lambda e, i: (0, 0))],
        out_specs=pl.BlockSpec((1, tr, f2), lambda e, i: (e, i, 0)),
        out_shape=jax.ShapeDtypeStruct((ne, d, f2), BF16),
        compiler_params=_params(("arbitrary", "arbitrary")),
        name="moe_deinterleave",
    )(w_gu, perm)


def _route_tables(route, ne, tr, tm):
    t = route.shape[0]
    a = t * TOP_K
    eidx = route[:, :TOP_K].astype(jnp.int32).reshape(a)
    wts = route[:, TOP_K:2 * TOP_K].reshape(a)
    order = jnp.argsort(eidx, stable=True).astype(jnp.int32)
    sorted_e = eidx[order]
    counts = jnp.sum(eidx[:, None] == jnp.arange(ne, dtype=jnp.int32)[None, :], axis=0,
                     dtype=jnp.int32)
    starts = jnp.cumsum(counts) - counts
    padded = (counts + tr - 1) // tr * tr
    pends = jnp.cumsum(padded)
    dest = (pends - padded)[sorted_e] + jnp.arange(a, dtype=jnp.int32) - starts[sorted_e]
    p = a + ne * tr
    n_tiles = p // tr
    src_tok = jnp.zeros((p,), jnp.int32).at[dest].set(order // TOP_K)
    w_sorted = jnp.zeros((p,), F32).at[dest].set(wts[order])
    pos = jnp.zeros((a,), jnp.int32).at[order].set(dest)
    used = (pends[-1] // tr).astype(jnp.int32)
    tile_e = jnp.searchsorted(pends, jnp.arange(n_tiles, dtype=jnp.int32) * tr,
                              side="right").astype(jnp.int32)
    tile_e = jnp.where(jnp.arange(n_tiles) < used, tile_e, tile_e[used - 1])
    pos_blk = pos.reshape(t // tm, tm, TOP_K).transpose(0, 2, 1).reshape(a)
    return src_tok, w_sorted.reshape(p, 1), tile_e, used.reshape(1), pos_blk


def _moe_kernel(tile_e_ref, src_ref, used_ref, h_hbm, wgl_ref, wd_ref, bg_ref, bl_ref, bd_ref,
                ws_ref, o_ref, buf, sem, *, tr):
    t = pl.program_id(0)
    used = used_ref[0]
    slot = t % 2
    f = bg_ref.shape[-1]

    @pl.when(t == 0)
    def _():
        _gather_start(h_hbm, src_ref, 0, tr, buf.at[0], sem.at[0])

    @pl.when(t + 1 < used)
    def _():
        _gather_start(h_hbm, src_ref, (t + 1) * tr, tr, buf.at[1 - slot], sem.at[1 - slot])

    @pl.when(t < used)
    def _():
        _gather_wait(h_hbm, tr, buf.at[slot], sem.at[slot])
        gl = jnp.dot(buf[slot].astype(BF16), wgl_ref[0], preferred_element_type=F32)
        xg = jnp.minimum(gl[:, :f] + bg_ref[0], SWIGLU_LIMIT)
        xl = jnp.clip(gl[:, f:] + bl_ref[0], -SWIGLU_LIMIT, SWIGLU_LIMIT)
        act = xg * _sigmoid(SWIGLU_ALPHA * xg) * (xl + 1.0)
        ws = ws_ref[...]
        o_ref[...] = (jnp.dot((act * ws).astype(BF16), wd_ref[0, 0].astype(BF16),
                              preferred_element_type=F32) + ws * bd_ref[0])

    @pl.when(t >= used)
    def _():
        o_ref[...] = jnp.zeros(o_ref.shape, F32)


def _moe_sparse(h, w_gl, w_down, layer, b_g, b_l, b_down, src_tok, w_sorted, tile_e, used, tr):
    t, d = h.shape
    ne, _, f2 = w_gl.shape
    f = f2 // 2
    p = src_tok.shape[0]
    return pl.pallas_call(
        functools.partial(_moe_kernel, tr=tr),
        grid_spec=pltpu.PrefetchScalarGridSpec(
            num_scalar_prefetch=3,
            grid=(p // tr,),
            in_specs=[
                pl.BlockSpec(memory_space=pl.ANY),
                pl.BlockSpec((1, d, f2), lambda i, te, *_: (te[i], 0, 0)),
                pl.BlockSpec((1, 1, f, d), lambda i, te, *_: (layer, te[i], 0, 0)),
                pl.BlockSpec((1, 1, f), lambda i, te, *_: (te[i], 0, 0)),
                pl.BlockSpec((1, 1, f), lambda i, te, *_: (te[i], 0, 0)),
                pl.BlockSpec((1, 1, d), lambda i, te, *_: (te[i], 0, 0)),
                pl.BlockSpec((tr, 1), lambda i, *_: (i, 0)),
            ],
            out_specs=pl.BlockSpec((tr, d), lambda i, *_: (i, 0)),
            scratch_shapes=[pltpu.VMEM((2, tr, d), F32), pltpu.SemaphoreType.DMA((2,))],
        ),
        out_shape=jax.ShapeDtypeStruct((p, d), F32),
        compiler_params=_params(("arbitrary",)),
        name="moe_experts",
    )(tile_e, src_tok, used, h, w_gl, w_down, b_g, b_l, b_down, w_sorted)


def kernel(x, c, w_cond, b_cond, w_mod, b_mod, g_mix_pre, g_mix_post, g_ffn_pre, g_ffn_post, attn_w_in, attn_w_out, ssd_w_in, ssd_conv_w, ssd_conv_b, ssd_dt_bias, ssd_a_log, ssd_d, ssd_norm_w, ssd_w_out, moe_w_router, moe_b_router, moe_w_gu, moe_b_gu, moe_w_down, moe_b_down):
    bsz, seq, d = x.shape
    assert bsz == 1, "kernel is written for a single sequence"
    depth = w_mod.shape[0]
    ne = moe_w_router.shape[-1]
    ff = moe_w_down.shape[2]
    d_inner = ssd_w_out.shape[1]
    conv_dim = ssd_conv_w.shape[-1]

    mod = _conditioning(c, w_cond, b_cond, w_mod, b_mod)

    def mods(i):
        return [mod[i, k * d:(k + 1) * d] for k in range(N_MOD)]

    xt = x.reshape(seq, d)
    tm = _tile(seq, GLUE_ROWS)
    sh_m, sc_m, ga_m, sh_f, sc_f, ga_f = mods(0)
    (h,) = _glue(xt, nxt=(g_mix_pre[0], sc_m, sh_m), tm=tm)

    for i in range(depth):
        j = i // 2
        if i % 2 == 0:
            q = _matmul(h, attn_w_in, j, BF16, col0=0, ncols=d,
                        scale=-LOG2E * SB_HEAD_DIM ** -0.5)
            kv = _matmul(h, attn_w_in, j, BF16, col0=d, ncols=2 * d)
            o = _attention(q, kv)
            y = _matmul(o, attn_w_out, j, F32)
        else:
            z = _matmul(h, ssd_w_in, j, F32, col0=0, ncols=d_inner)
            xbc = _matmul(h, ssd_w_in, j, F32, col0=d_inner, ncols=conv_dim)
            dt_raw = _matmul(h, ssd_w_in, j, F32, col0=d_inner + conv_dim)
            xbc = _conv_silu(xbc, ssd_conv_w[j], ssd_conv_b[j])
            dtr, acsr, acsc = _dtprep(dt_raw, ssd_dt_bias[j], ssd_a_log[j])
            d_exp = jnp.repeat(ssd_d[j], SSD_HEAD_DIM).reshape(1, d_inner)
            yn = _ssd_scan(xbc, z, dtr, acsr, acsc, d_exp, ssd_norm_w[j])
            y = _matmul(yn, ssd_w_out, j, F32)

        wr = jnp.pad(moe_w_router[i], ((0, 0), (0, LANES - ne)))
        br = jnp.pad(moe_b_router[i], (0, LANES - ne), constant_values=NEG_INF).reshape(1, LANES)
        xt, h, route = _glue(xt, resid=(y, ga_m, g_mix_post[i]), nxt=(g_ffn_pre[i], sc_f, sh_f),
                             router=(wr, br), h_dtype=F32, tm=tm)

        src_tok, w_sorted, tile_e, used, pos_blk = _route_tables(route, ne, MOE_TILE_ROWS, tm)
        w_gl = _deinterleave_cast(moe_w_gu, i)
        b_g = moe_b_gu[i][:, 0::2].reshape(ne, 1, ff)
        b_l = moe_b_gu[i][:, 1::2].reshape(ne, 1, ff)
        y_rows = _moe_sparse(h, w_gl, moe_w_down, i, b_g, b_l, moe_b_down[i].reshape(ne, 1, d),
                             src_tok, w_sorted, tile_e, used, MOE_TILE_ROWS)

        ga_prev = ga_f
        if i + 1 < depth:
            sh_m, sc_m, ga_m, sh_f, sc_f, ga_f = mods(i + 1)
            xt, h = _glue(xt, resid=(y_rows, ga_prev, g_ffn_post[i]),
                          nxt=(g_mix_pre[i + 1], sc_m, sh_m), gather_pos=pos_blk, tm=tm)
        else:
            (xt,) = _glue(xt, resid=(y_rows, ga_prev, g_ffn_post[i]), gather_pos=pos_blk, tm=tm)

    return xt.reshape(bsz, seq, d)
```

```python
import functools
import math

import jax
import jax.numpy as jnp
from jax import lax
from jax.experimental import pallas as pl
from jax.experimental.pallas import tpu as pltpu

SB_HEAD_DIM = 128
SSD_HEAD_DIM = 64
SSD_GROUPS = 8
SSD_STATE = 128
SSD_CONV = 4
SSD_CHUNK = 128
TOP_K = 4
SWIGLU_ALPHA = 1.702
SWIGLU_LIMIT = 7.0
N_MOD = 6
NORM_EPS = 1e-6

LANES = 128
SUBLANES = 8
VMEM_LIMIT_BYTES = 56 * 1024 * 1024

GLUE_ROWS = 128
MOE_TILE_ROWS = 256
GATHER_UNROLL = 8

F32 = jnp.float32
BF16 = jnp.bfloat16
HIGHEST = lax.Precision.HIGHEST
NEG_INF = float("-inf")
LOG2E = math.log2(math.e)


def _params(sem):
    return pltpu.CompilerParams(dimension_semantics=sem, vmem_limit_bytes=VMEM_LIMIT_BYTES)


def _sigmoid(x):
    return 1.0 / (1.0 + jnp.exp(-x))


def _softplus(x):
    return jnp.maximum(x, 0.0) + jnp.log(1.0 + jnp.exp(-jnp.abs(x)))


def _tile(n, pref):
    if n <= pref:
        return n
    t = pref
    while n % t:
        t //= 2
    return t


def _cond_kernel(c_ref, w_ref, b_ref, o_ref):
    c = c_ref[...]
    s = c * _sigmoid(c)
    r = jnp.dot(s, w_ref[...], precision=HIGHEST, preferred_element_type=F32) + b_ref[...]
    o_ref[...] = r * _sigmoid(r)


def _mod_kernel(cond_ref, w_ref, b_ref, o_ref):
    o_ref[0] = jnp.dot(cond_ref[...], w_ref[0], precision=HIGHEST,
                       preferred_element_type=F32) + b_ref[0]


def _conditioning(c, w_cond, b_cond, w_mod, b_mod):
    bsz, d = c.shape
    rank = w_cond.shape[1]
    depth, _, nmod = w_mod.shape
    rows = SUBLANES
    c8 = jnp.broadcast_to(c[:1], (rows, d))
    cond = pl.pallas_call(
        _cond_kernel,
        out_shape=jax.ShapeDtypeStruct((rows, rank), F32),
        compiler_params=_params(None),
        name="cond_trunk",
    )(c8, w_cond, b_cond.reshape(1, rank))
    tn = _tile(nmod, 4096)
    mod = pl.pallas_call(
        _mod_kernel,
        grid=(depth, nmod // tn),
        in_specs=[
            pl.BlockSpec((rows, rank), lambda i, j: (0, 0)),
            pl.BlockSpec((1, rank, tn), lambda i, j: (i, 0, j)),
            pl.BlockSpec((1, 1, tn), lambda i, j: (i, 0, j)),
        ],
        out_specs=pl.BlockSpec((1, rows, tn), lambda i, j: (i, 0, j)),
        out_shape=jax.ShapeDtypeStruct((depth, rows, nmod), F32),
        compiler_params=_params(("arbitrary", "arbitrary")),
        name="cond_mod",
    )(cond, w_mod, b_mod.reshape(depth, 1, nmod))
    return mod[:, 0, :]


def _rms(y):
    return y * lax.rsqrt(jnp.mean(y * y, axis=-1, keepdims=True) + NORM_EPS)


def _pack_bf16_pairs(v):
    half = v.shape[1] // 2
    bits = lax.bitcast_convert_type(v.astype(BF16).astype(F32), jnp.uint32)
    return (bits[:, :half] >> 16) | bits[:, half:]


def _unpack_bf16_pairs(w):
    lo = lax.bitcast_convert_type(w << 16, F32)
    hi = lax.bitcast_convert_type(w & jnp.uint32(0xFFFF0000), F32)
    return jnp.concatenate([lo, hi], axis=1)


def _row_copy(src_hbm, row, dst, r, sem):
    return pltpu.make_async_copy(src_hbm.at[pl.ds(row, 1)], dst.at[pl.ds(r, 1)], sem)


def _gather_start(src_hbm, idx_ref, base, n, dst, sem):
    def body(r, carry):
        _row_copy(src_hbm, idx_ref[base + r], dst, r, sem).start()
        return carry

    lax.fori_loop(0, n, body, 0, unroll=GATHER_UNROLL)


def _gather_wait(src_hbm, n, dst, sem):
    def body(r, carry):
        _row_copy(src_hbm, 0, dst, r, sem).wait()
        return carry

    lax.fori_loop(0, n, body, 0, unroll=GATHER_UNROLL)


def _glue_kernel(*refs, has_resid, has_next, has_router, gathered):
    it = iter(refs)
    if gathered:
        pos_ref = next(it)
    x_ref = next(it)
    if gathered:
        rw_ref = next(it)
    if has_resid:
        y_ref, ga_ref, gpost_ref = next(it), next(it), next(it)
    if has_next:
        gpre_ref, sc_ref, sh_ref = next(it), next(it), next(it)
    if has_router:
        wr_ref, br_ref = next(it), next(it)
    if has_resid:
        xo_ref = next(it)
    if has_next:
        h_ref = next(it)
    if has_router:
        route_ref = next(it)
    if gathered:
        buf, sem = next(it), next(it)

    x = x_ref[...]
    if has_resid:
        if gathered:
            i = pl.program_id(0)
            tm = x.shape[0]
            nrow = TOP_K * tm
            slot = i % 2

            @pl.when(i == 0)
            def _():
                _gather_start(y_ref, pos_ref, 0, nrow, buf.at[0], sem.at[0])

            @pl.when(i + 1 < pl.num_programs(0))
            def _():
                _gather_start(y_ref, pos_ref, (i + 1) * nrow, nrow, buf.at[1 - slot],
                              sem.at[1 - slot])

            _gather_wait(y_ref, nrow, buf.at[slot], sem.at[slot])
            rw = rw_ref[...]
            y = rw[:, TOP_K:TOP_K + 1] * _unpack_bf16_pairs(buf[slot, 0:tm, :])
            for k in range(1, TOP_K):
                y = y + (rw[:, TOP_K + k:TOP_K + k + 1]
                         * _unpack_bf16_pairs(buf[slot, k * tm:(k + 1) * tm, :]))
        else:
            y = y_ref[...]
        x = x + ga_ref[...] * (_rms(y) * gpost_ref[...])
        xo_ref[...] = x
    if has_next:
        h = (_rms(x) * gpre_ref[...]) * (1.0 + sc_ref[...]) + sh_ref[...]
        if has_router:
            h_ref[...] = _pack_bf16_pairs(h)
        else:
            h_ref[...] = h.astype(h_ref.dtype)
    if has_router:
        logits = jnp.dot(h, wr_ref[...], precision=HIGHEST,
                         preferred_element_type=F32) + br_ref[...]
        lane = lax.broadcasted_iota(jnp.int32, logits.shape, 1).astype(F32)
        work = logits
        vals, idxs = [], []
        for _ in range(TOP_K):
            m = jnp.max(work, axis=1, keepdims=True)
            idx = jnp.min(jnp.where(work == m, lane, float(LANES)), axis=1, keepdims=True)
            vals.append(m)
            idxs.append(idx)
            work = jnp.where(lane == idx, NEG_INF, work)
        exps = [jnp.exp(v - vals[0]) for v in vals]
        denom = exps[0]
        for e in exps[1:]:
            denom = denom + e
        route = jnp.zeros_like(logits)
        for j in range(TOP_K):
            route = jnp.where(lane == float(j), idxs[j], route)
            route = jnp.where(lane == float(TOP_K + j), exps[j] / denom, route)
        route_ref[...] = route


def _glue(x, resid=None, nxt=None, router=None, gather=None, tm=GLUE_ROWS):
    t, d = x.shape
    tm = _tile(t, tm)
    gathered = gather is not None
    row = pl.BlockSpec((tm, d), lambda i, *_: (i, 0))
    vec = pl.BlockSpec((1, d), lambda i, *_: (0, 0))
    lanes = pl.BlockSpec((tm, LANES), lambda i, *_: (i, 0))
    args, in_specs, out_shape, out_specs, scratch = [x], [row], [], [], []
    if gathered:
        args.append(gather[1])
        in_specs.append(lanes)
    if resid is not None:
        y, ga, gpost = resid
        args += [y, ga.reshape(1, d), gpost.reshape(1, d)]
        in_specs += [pl.BlockSpec(memory_space=pl.ANY) if gathered else row, vec, vec]
        out_shape.append(jax.ShapeDtypeStruct((t, d), F32))
        out_specs.append(row)
    if nxt is not None:
        gpre, sc, sh = nxt
        args += [gpre.reshape(1, d), sc.reshape(1, d), sh.reshape(1, d)]
        in_specs += [vec, vec, vec]
        if router is None:
            out_shape.append(jax.ShapeDtypeStruct((t, d), BF16))
            out_specs.append(row)
        else:
            out_shape.append(jax.ShapeDtypeStruct((t, d // 2), jnp.uint32))
            out_specs.append(pl.BlockSpec((tm, d // 2), lambda i, *_: (i, 0)))
    if router is not None:
        wr, br = router
        args += [wr, br]
        in_specs += [pl.BlockSpec((d, LANES), lambda i, *_: (0, 0)),
                     pl.BlockSpec((1, LANES), lambda i, *_: (0, 0))]
        out_shape.append(jax.ShapeDtypeStruct((t, LANES), F32))
        out_specs.append(lanes)
    if gathered:
        args = [gather[0]] + args
        scratch = [pltpu.VMEM((2, TOP_K * tm, d // 2), jnp.uint32),
                   pltpu.SemaphoreType.DMA((2,))]
    outs = pl.pallas_call(
        functools.partial(_glue_kernel, has_resid=resid is not None, has_next=nxt is not None,
                          has_router=router is not None, gathered=gathered),
        grid_spec=pltpu.PrefetchScalarGridSpec(
            num_scalar_prefetch=1 if gathered else 0,
            grid=(t // tm,),
            in_specs=in_specs,
            out_specs=out_specs,
            scratch_shapes=scratch,
        ),
        out_shape=out_shape,
        compiler_params=_params(("arbitrary",)),
        name="glue_gather" if gathered else "glue",
    )(*args)
    return outs


def _mm_kernel(a_ref, w_ref, o_ref, *scratch, nk, scale):
    part = jnp.dot(a_ref[...], w_ref[0].astype(BF16), preferred_element_type=F32)
    if nk == 1:
        if scale != 1.0:
            part = part * scale
        o_ref[...] = part.astype(o_ref.dtype)
        return
    acc_ref = scratch[0] if scratch else o_ref
    k = pl.program_id(2)

    @pl.when(k == 0)
    def _():
        acc_ref[...] = part

    @pl.when(k > 0)
    def _():
        acc_ref[...] += part

    if scratch or scale != 1.0:
        @pl.when(k == nk - 1)
        def _():
            r = acc_ref[...]
            if scale != 1.0:
                r = r * scale
            o_ref[...] = r.astype(o_ref.dtype)


def _matmul(a, w, layer, out_dtype, col0=0, ncols=None, scale=1.0, tm=1024, tn=512, tk=4096):
    m, k = a.shape
    n = w.shape[2] - col0 if ncols is None else ncols
    tm, tn, tk = _tile(m, tm), _tile(n, tn), _tile(k, tk)
    assert col0 % tn == 0 and n % tn == 0
    nk = k // tk
    jb = col0 // tn
    scratch = []
    if nk > 1 and out_dtype != F32:
        scratch = [pltpu.VMEM((tm, tn), F32)]
    return pl.pallas_call(
        functools.partial(_mm_kernel, nk=nk, scale=scale),
        grid=(m // tm, n // tn, nk),
        in_specs=[pl.BlockSpec((tm, tk), lambda i, j, kk: (i, kk)),
                  pl.BlockSpec((1, tk, tn), lambda i, j, kk: (layer, kk, jb + j))],
        out_specs=pl.BlockSpec((tm, tn), lambda i, j, kk: (i, j)),
        out_shape=jax.ShapeDtypeStruct((m, n), out_dtype),
        scratch_shapes=scratch,
        compiler_params=_params(("arbitrary", "arbitrary", "arbitrary")),
        name="matmul",
    )(a, w)


def _attn_kernel(q_ref, k_ref, v_ref, o_ref, acc_ref, carry_ref, *, tq, tk):
    i = pl.program_id(1)
    r2 = lax.broadcasted_iota(jnp.int32, (tk, tk), 0)
    c2 = lax.broadcasted_iota(jnp.int32, (tk, tk), 1)
    later = jnp.where(r2 > c2, 1.0, 0.0).astype(BF16)

    acc_ref[...] = jnp.zeros(acc_ref.shape, F32)
    carry_ref[...] = jnp.zeros(carry_ref.shape, F32)

    def span(k_start, nsub, r0, diag):
        kk = k_ref[pl.ds(k_start, nsub * tk), :]
        vv = v_ref[pl.ds(k_start, nsub * tk), :]
        n = lax.dot_general(q_ref[r0:, :], kk, (((1,), (1,)), ((), ())),
                            preferred_element_type=F32)
        neg_abs = lax.bitcast_convert_type(
            lax.bitcast_convert_type(n, jnp.uint32) | jnp.uint32(0x80000000), F32)
        lk = jnp.minimum(n, 0.0) - jnp.log(1.0 + jnp.exp2(neg_abs)) * LOG2E
        if diag:
            row = lax.broadcasted_iota(jnp.int32, n.shape, 0)
            col = lax.broadcasted_iota(jnp.int32, n.shape, 1)
            vis = col < row
            lk = jnp.where(vis, lk, 0.0)
        carry = carry_ref[r0:, :]
        ws = [None] * nsub
        for s in range(nsub - 1, -1, -1):
            lk_s = lk[:, s * tk:(s + 1) * tk]
            la = jnp.dot(lk_s.astype(BF16), later, preferred_element_type=F32) + carry
            ws[s] = jnp.exp2((la + lk_s) - n[:, s * tk:(s + 1) * tk])
            carry = la[:, 0:1] + lk_s[:, 0:1]
        w = ws[0] if nsub == 1 else jnp.concatenate(ws, axis=1)
        if diag:
            w = jnp.where(vis, w, 0.0)
        acc_ref[r0:, :] += jnp.dot(w.astype(BF16), vv, preferred_element_type=F32)
        carry_ref[r0:, :] = carry

    q0 = i * tq
    for d in range(tq // tk - 1, -1, -1):
        span(pl.multiple_of(q0 + d * tk, tk), 1, d * tk, True)

    nsub = math.gcd(tq // tk, 4)
    nstep = q0 // (nsub * tk)

    def body(t, _):
        span(pl.multiple_of((nstep - 1 - t) * (nsub * tk), nsub * tk), nsub, 0, False)
        return 0

    lax.fori_loop(0, nstep, body, 0)
    o_ref[...] = acc_ref[...].astype(o_ref.dtype)


def _attention(q, kv, tq=1024, tk=256):
    l, d = q.shape
    nh = d // SB_HEAD_DIM
    tq = _tile(l, tq)
    tk = _tile(tq, tk)
    return pl.pallas_call(
        functools.partial(_attn_kernel, tq=tq, tk=tk),
        grid=(nh, l // tq),
        in_specs=[
            pl.BlockSpec((tq, SB_HEAD_DIM), lambda h, i: (i, h)),
            pl.BlockSpec((l, SB_HEAD_DIM), lambda h, i: (0, h)),
            pl.BlockSpec((l, SB_HEAD_DIM), lambda h, i: (0, nh + h)),
        ],
        out_specs=pl.BlockSpec((tq, SB_HEAD_DIM), lambda h, i: (i, h)),
        out_shape=jax.ShapeDtypeStruct((l, d), BF16),
        scratch_shapes=[pltpu.VMEM((tq, SB_HEAD_DIM), F32), pltpu.VMEM((tq, 1), F32)],
        compiler_params=_params(("arbitrary", "arbitrary")),
        name="sb_attention",
    )(q, kv, kv)


def _conv_kernel(u_ref, w_ref, b_ref, o_ref, ext_ref, *, tl):
    halo = SUBLANES

    @pl.when(pl.program_id(1) == 0)
    def _():
        ext_ref[0:halo, :] = jnp.zeros((halo, ext_ref.shape[1]), F32)

    u = u_ref[...]
    ext_ref[halo:halo + tl, :] = u
    acc = b_ref[...] + w_ref[SSD_CONV - 1:SSD_CONV, :] * u
    for k in range(SSD_CONV - 1):
        off = halo - (SSD_CONV - 1) + k
        acc = acc + w_ref[k:k + 1, :] * ext_ref[off:off + tl, :]
    o_ref[...] = acc * _sigmoid(acc)
    ext_ref[0:halo, :] = u[tl - halo:tl, :]


def _conv_silu(u, w, b, tl=512, tc=1024):
    l, c = u.shape
    tl, tc = _tile(l, tl), _tile(c, tc)
    return pl.pallas_call(
        functools.partial(_conv_kernel, tl=tl),
        grid=(c // tc, l // tl),
        in_specs=[
            pl.BlockSpec((tl, tc), lambda j, i: (i, j)),
            pl.BlockSpec((SSD_CONV, tc), lambda j, i: (0, j)),
            pl.BlockSpec((1, tc), lambda j, i: (0, j)),
        ],
        out_specs=pl.BlockSpec((tl, tc), lambda j, i: (i, j)),
        out_shape=jax.ShapeDtypeStruct((l, c), F32),
        scratch_shapes=[pltpu.VMEM((tl + SUBLANES, tc), F32)],
        compiler_params=_params(("arbitrary", "arbitrary")),
        name="ssd_conv",
    )(u, w, b.reshape(1, c))


def _split3(x):
    p1 = x.astype(BF16)
    r1 = x - p1.astype(F32)
    p2 = r1.astype(BF16)
    p3 = (r1 - p2.astype(F32)).astype(BF16)
    return p1, p2, p3


def _dtprep_kernel(dt_ref, bias_ref, alog_ref, dtr_ref, acsr_ref, acsc_ref, *, hpg):
    dt = _softplus(dt_ref[...] + bias_ref[...])
    adt = dt * (-jnp.exp(alog_ref[...]))
    n = dt.shape[0]
    r = lax.broadcasted_iota(jnp.int32, (n, n), 0)
    c = lax.broadcasted_iota(jnp.int32, (n, n), 1)
    upto = jnp.where(c <= r, 1.0, 0.0).astype(BF16)
    acs = sum(jnp.dot(upto, p, preferred_element_type=F32) for p in _split3(adt))
    dtr_ref[...] = dt.T
    acsr_ref[...] = acs.T
    for g in range(acsc_ref.shape[0]):
        acsc_ref[g] = acs[:, g * hpg:(g + 1) * hpg]


def _dtprep(dt_raw, dt_bias, a_log):
    l, nh = dt_raw.shape
    hpg = nh // SSD_GROUPS
    ch = SSD_CHUNK
    return pl.pallas_call(
        functools.partial(_dtprep_kernel, hpg=hpg),
        grid=(l // ch,),
        in_specs=[
            pl.BlockSpec((ch, nh), lambda i: (i, 0)),
            pl.BlockSpec((1, nh), lambda i: (0, 0)),
            pl.BlockSpec((1, nh), lambda i: (0, 0)),
        ],
        out_specs=[
            pl.BlockSpec((nh, ch), lambda i: (0, i)),
            pl.BlockSpec((nh, ch), lambda i: (0, i)),
            pl.BlockSpec((SSD_GROUPS, ch, hpg), lambda i: (0, i, 0)),
        ],
        out_shape=[
            jax.ShapeDtypeStruct((nh, l), F32),
            jax.ShapeDtypeStruct((nh, l), F32),
            jax.ShapeDtypeStruct((SSD_GROUPS, l, hpg), F32),
        ],
        compiler_params=_params(("arbitrary",)),
        name="ssd_dtprep",
    )(dt_raw, dt_bias.reshape(1, nh), a_log.reshape(1, nh))


def _ssd_kernel(xs_ref, b_ref, c_ref, dtr_ref, acsr_ref, acsc_ref, z_ref, dexp_ref, nw_ref,
                o_ref, st_ref, y_ref, *, npairs):
    ch = SSD_CHUNK
    hp = SSD_HEAD_DIM

    @pl.when(pl.program_id(1) == 0)
    def _():
        st_ref[...] = jnp.zeros(st_ref.shape, F32)

    bm = b_ref[...]
    cm = c_ref[...]
    cb = lax.dot_general(cm.astype(BF16), bm.astype(BF16), (((1,), (1,)), ((), ())),
                         preferred_element_type=F32)
    bt = bm.T
    row = lax.broadcasted_iota(jnp.int32, (ch, ch), 0)
    col = lax.broadcasted_iota(jnp.int32, (ch, ch), 1)
    causal = col <= row
    first = lax.broadcasted_iota(jnp.int32, (ch, 2 * hp), 1) < hp
    dtr = dtr_ref[...]
    acsr = acsr_ref[...]
    acsc = acsc_ref[0]

    for pr in range(npairs):
        sl = slice(pr * 2 * hp, (pr + 1) * 2 * hp)
        xs_b = xs_ref[:, sl].astype(BF16)
        prev = st_ref[pr]
        rhs = jnp.concatenate([xs_b, prev.astype(BF16)], axis=0)
        outs, news = [], []
        for hh in (2 * pr, 2 * pr + 1):
            a_col = jnp.broadcast_to(acsc[:, hh:hh + 1], (ch, ch))
            a_row = acsr[hh:hh + 1, :]
            dt_row = dtr[hh:hh + 1, :]
            lmat = jnp.exp(jnp.where(causal, a_col - a_row, NEG_INF))
            scores = cb * lmat * dt_row
            c_dec = cm * jnp.exp(a_col)
            lhs = jnp.concatenate([scores.astype(BF16), c_dec.astype(BF16)], axis=1)
            outs.append(jnp.dot(lhs, rhs, preferred_element_type=F32))
            a_last = a_row[:, ch - 1:ch]
            b_dec = (bt * (dt_row * jnp.exp(a_last - a_row))).astype(BF16)
            news.append(jnp.exp(a_last) * prev
                        + jnp.dot(b_dec, xs_b, preferred_element_type=F32))
        y_ref[:, sl] = jnp.where(first, outs[0], outs[1])
        st_ref[pr] = jnp.where(first, news[0], news[1])

    y = y_ref[...] + dexp_ref[...] * xs_ref[...]
    z = z_ref[...]
    yg = y * (z * _sigmoid(z))
    o_ref[...] = (_rms(yg) * nw_ref[...]).astype(o_ref.dtype)


def _ssd_scan(xbc, z, dtr, acsr, acsc, d_exp, norm_w):
    l, d_inner = z.shape
    gw = d_inner // SSD_GROUPS
    npairs = gw // (2 * SSD_HEAD_DIM)
    hpg = gw // SSD_HEAD_DIM
    ch = SSD_CHUNK
    nb = d_inner // SSD_STATE
    return pl.pallas_call(
        functools.partial(_ssd_kernel, npairs=npairs),
        grid=(SSD_GROUPS, l // ch),
        in_specs=[
            pl.BlockSpec((ch, gw), lambda g, c: (c, g)),
            pl.BlockSpec((ch, SSD_STATE), lambda g, c: (c, nb + g)),
            pl.BlockSpec((ch, SSD_STATE), lambda g, c: (c, nb + SSD_GROUPS + g)),
            pl.BlockSpec((hpg, ch), lambda g, c: (g, c)),
            pl.BlockSpec((hpg, ch), lambda g, c: (g, c)),
            pl.BlockSpec((1, ch, hpg), lambda g, c: (g, c, 0)),
            pl.BlockSpec((ch, gw), lambda g, c: (c, g)),
            pl.BlockSpec((1, gw), lambda g, c: (0, g)),
            pl.BlockSpec((1, gw), lambda g, c: (0, g)),
        ],
        out_specs=pl.BlockSpec((ch, gw), lambda g, c: (c, g)),
        out_shape=jax.ShapeDtypeStruct((l, d_inner), BF16),
        scratch_shapes=[pltpu.VMEM((npairs, SSD_STATE, 2 * SSD_HEAD_DIM), F32),
                        pltpu.VMEM((ch, gw), F32)],
        compiler_params=_params(("arbitrary", "arbitrary")),
        name="ssd_scan",
    )(xbc, xbc, xbc, dtr, acsr, acsc, z, d_exp, norm_w.reshape(1, d_inner))


def _deinterleave_kernel(w_ref, p_ref, o_ref):
    o_ref[0] = jnp.dot(w_ref[0, 0].astype(BF16), p_ref[...],
                       preferred_element_type=F32).astype(BF16)


def _deinterleave_cast(w_gu, layer, tr=1024):
    _, ne, d, f2 = w_gu.shape
    tr = _tile(d, tr)
    src = jnp.arange(f2)
    dst = jnp.where(src % 2 == 0, src // 2, f2 // 2 + src // 2)
    perm = (dst[:, None] == jnp.arange(f2)[None, :]).astype(BF16)
    return pl.pallas_call(
        _deinterleave_kernel,
        grid=(ne, d // tr),
        in_specs=[pl.BlockSpec((1, 1, tr, f2), lambda e, i: (layer, e, i, 0)),
                  pl.BlockSpec((f2, f2), lambda e, i: (0, 0))],
        out_specs=pl.BlockSpec((1, tr, f2), lambda e, i: (e, i, 0)),
        out_shape=jax.ShapeDtypeStruct((ne, d, f2), BF16),
        compiler_params=_params(("arbitrary", "arbitrary")),
        name="moe_deinterleave",
    )(w_gu, perm)


def _lookup(table, idx):
    hit = idx[:, None] == jnp.arange(table.shape[0], dtype=jnp.int32)[None, :]
    return jnp.sum(jnp.where(hit, table[None, :], 0), axis=1)


def _route_tables(route, ne, tr, tm):
    t = route.shape[0]
    a = t * TOP_K
    eidx = route[:, :TOP_K].astype(jnp.int32).reshape(a)
    order = jnp.argsort(eidx, stable=True).astype(jnp.int32)
    rank = jnp.argsort(order).astype(jnp.int32)
    counts = jnp.sum(eidx[:, None] == jnp.arange(ne, dtype=jnp.int32)[None, :], axis=0,
                     dtype=jnp.int32)
    starts = jnp.cumsum(counts) - counts
    padded = (counts + tr - 1) // tr * tr
    pends = jnp.cumsum(padded)
    pstarts = pends - padded
    pos = rank + _lookup(pstarts - starts, eidx)
    n_tiles = (a + ne * tr) // tr
    used = (pends[-1] // tr).astype(jnp.int32)
    tile0 = jnp.arange(n_tiles, dtype=jnp.int32) * tr
    tile0_used = jnp.minimum(tile0, (used - 1) * tr)
    tile_e = jnp.sum(pends[None, :] <= tile0_used[:, None], axis=1, dtype=jnp.int32)
    src_base = jnp.clip(_lookup(starts - pstarts, tile_e) + tile0_used, 0, a)
    order_tok = jnp.concatenate([order // TOP_K, jnp.zeros((tr,), jnp.int32)])
    pos_blk = pos.reshape(t // tm, tm, TOP_K).transpose(0, 2, 1).reshape(a)
    return order_tok, tile_e, src_base, used.reshape(1), pos_blk


def _moe_kernel(tile_e_ref, base_ref, used_ref, tok_ref, h_hbm, wgl_ref, wd_ref, bg_ref, bl_ref,
                bd_ref, o_ref, buf, sem, *, tr):
    t = pl.program_id(0)
    used = used_ref[0]
    slot = t % 2
    f = bg_ref.shape[-1]

    @pl.when(t == 0)
    def _():
        _gather_start(h_hbm, tok_ref, base_ref[0], tr, buf.at[0], sem.at[0])

    @pl.when(t + 1 < used)
    def _():
        _gather_start(h_hbm, tok_ref, base_ref[t + 1], tr, buf.at[1 - slot], sem.at[1 - slot])

    @pl.when(t < used)
    def _():
        _gather_wait(h_hbm, tr, buf.at[slot], sem.at[slot])
        gl = jnp.dot(_unpack_bf16_pairs(buf[slot]).astype(BF16), wgl_ref[0],
                     preferred_element_type=F32)
        xg = jnp.minimum(gl[:, :f] + bg_ref[0], SWIGLU_LIMIT)
        xl = jnp.clip(gl[:, f:] + bl_ref[0], -SWIGLU_LIMIT, SWIGLU_LIMIT)
        act = xg * _sigmoid(SWIGLU_ALPHA * xg) * (xl + 1.0)
        o_ref[...] = _pack_bf16_pairs(jnp.dot(act.astype(BF16), wd_ref[0, 0].astype(BF16),
                                              preferred_element_type=F32) + bd_ref[0])

    @pl.when(t >= used)
    def _():
        o_ref[...] = jnp.zeros(o_ref.shape, o_ref.dtype)


def _moe_sparse(h, w_gl, w_down, layer, b_g, b_l, b_down, order_tok, tile_e, src_base, used, tr):
    ne, d, f2 = w_gl.shape
    f = f2 // 2
    n_tiles = tile_e.shape[0]
    return pl.pallas_call(
        functools.partial(_moe_kernel, tr=tr),
        grid_spec=pltpu.PrefetchScalarGridSpec(
            num_scalar_prefetch=4,
            grid=(n_tiles,),
            in_specs=[
                pl.BlockSpec(memory_space=pl.ANY),
                pl.BlockSpec((1, d, f2), lambda i, te, *_: (te[i], 0, 0)),
                pl.BlockSpec((1, 1, f, d), lambda i, te, *_: (layer, te[i], 0, 0)),
                pl.BlockSpec((1, 1, f), lambda i, te, *_: (te[i], 0, 0)),
                pl.BlockSpec((1, 1, f), lambda i, te, *_: (te[i], 0, 0)),
                pl.BlockSpec((1, 1, d), lambda i, te, *_: (te[i], 0, 0)),
            ],
            out_specs=pl.BlockSpec((tr, d // 2), lambda i, *_: (i, 0)),
            scratch_shapes=[pltpu.VMEM((2, tr, d // 2), jnp.uint32),
                            pltpu.SemaphoreType.DMA((2,))],
        ),
        out_shape=jax.ShapeDtypeStruct((n_tiles * tr, d // 2), jnp.uint32),
        compiler_params=_params(("arbitrary",)),
        name="moe_experts",
    )(tile_e, src_base, used, order_tok, h, w_gl, w_down, b_g, b_l, b_down)


def kernel(x, c, w_cond, b_cond, w_mod, b_mod, g_mix_pre, g_mix_post, g_ffn_pre, g_ffn_post, attn_w_in, attn_w_out, ssd_w_in, ssd_conv_w, ssd_conv_b, ssd_dt_bias, ssd_a_log, ssd_d, ssd_norm_w, ssd_w_out, moe_w_router, moe_b_router, moe_w_gu, moe_b_gu, moe_w_down, moe_b_down):
    bsz, seq, d = x.shape
    assert bsz == 1, "kernel is written for a single sequence"
    depth = w_mod.shape[0]
    ne = moe_w_router.shape[-1]
    ff = moe_w_down.shape[2]
    d_inner = ssd_w_out.shape[1]
    conv_dim = ssd_conv_w.shape[-1]

    mod = _conditioning(c, w_cond, b_cond, w_mod, b_mod)

    def mods(i):
        return [mod[i, k * d:(k + 1) * d] for k in range(N_MOD)]

    xt = x.reshape(seq, d)
    tm = _tile(seq, GLUE_ROWS)
    sh_m, sc_m, ga_m, sh_f, sc_f, ga_f = mods(0)
    (h,) = _glue(xt, nxt=(g_mix_pre[0], sc_m, sh_m), tm=tm)

    for i in range(depth):
        j = i // 2
        if i % 2 == 0:
            q = _matmul(h, attn_w_in, j, BF16, col0=0, ncols=d,
                        scale=-LOG2E * SB_HEAD_DIM ** -0.5)
            kv = _matmul(h, attn_w_in, j, BF16, col0=d, ncols=2 * d)
            o = _attention(q, kv)
            y = _matmul(o, attn_w_out, j, F32)
        else:
            z = _matmul(h, ssd_w_in, j, F32, col0=0, ncols=d_inner)
            xbc = _matmul(h, ssd_w_in, j, F32, col0=d_inner, ncols=conv_dim)
            dt_raw = _matmul(h, ssd_w_in, j, F32, col0=d_inner + conv_dim)
            xbc = _conv_silu(xbc, ssd_conv_w[j], ssd_conv_b[j])
            dtr, acsr, acsc = _dtprep(dt_raw, ssd_dt_bias[j], ssd_a_log[j])
            d_exp = jnp.repeat(ssd_d[j], SSD_HEAD_DIM).reshape(1, d_inner)
            yn = _ssd_scan(xbc, z, dtr, acsr, acsc, d_exp, ssd_norm_w[j])
            y = _matmul(yn, ssd_w_out, j, F32)

        wr = jnp.pad(moe_w_router[i], ((0, 0), (0, LANES - ne)))
        br = jnp.pad(moe_b_router[i], (0, LANES - ne), constant_values=NEG_INF).reshape(1, LANES)
        xt, h, route = _glue(xt, resid=(y, ga_m, g_mix_post[i]),
                             nxt=(g_ffn_pre[i], sc_f, sh_f), router=(wr, br), tm=tm)

        order_tok, tile_e, src_base, used, pos_blk = _route_tables(route, ne, MOE_TILE_ROWS, tm)
        w_gl = _deinterleave_cast(moe_w_gu, i)
        b_g = moe_b_gu[i][:, 0::2].reshape(ne, 1, ff)
        b_l = moe_b_gu[i][:, 1::2].reshape(ne, 1, ff)
        y_rows = _moe_sparse(h, w_gl, moe_w_down, i, b_g, b_l, moe_b_down[i].reshape(ne, 1, d),
                             order_tok, tile_e, src_base, used, MOE_TILE_ROWS)

        ga_prev = ga_f
        if i + 1 < depth:
            sh_m, sc_m, ga_m, sh_f, sc_f, ga_f = mods(i + 1)
            xt, h = _glue(xt, resid=(y_rows, ga_prev, g_ffn_post[i]),
                          nxt=(g_mix_pre[i + 1], sc_m, sh_m), gather=(pos_blk, route), tm=tm)
        else:
            (xt,) = _glue(xt, resid=(y_rows, ga_prev, g_ffn_post[i]), gather=(pos_blk, route),
                          tm=tm)

    return xt.reshape(bsz, seq, d)
```

```python
import functools
import math

import jax
import jax.numpy as jnp
from jax import lax
from jax.experimental import pallas as pl
from jax.experimental.pallas import tpu as pltpu

SB_HEAD_DIM = 128
SSD_HEAD_DIM = 64
SSD_GROUPS = 8
SSD_STATE = 128
SSD_CONV = 4
SSD_CHUNK = 128
TOP_K = 4
SWIGLU_ALPHA = 1.702
SWIGLU_LIMIT = 7.0
N_MOD = 6
NORM_EPS = 1e-6

LANES = 128
SUBLANES = 8
VMEM_LIMIT_BYTES = 56 * 1024 * 1024

GLUE_ROWS = 128
MOE_TILE_ROWS = 256
GATHER_UNROLL = 8

F32 = jnp.float32
BF16 = jnp.bfloat16
HIGHEST = lax.Precision.HIGHEST
NEG_INF = float("-inf")
LOG2E = math.log2(math.e)
F32_UNDERFLOW_LOG2 = -160.0


def _params(sem):
    return pltpu.CompilerParams(dimension_semantics=sem, vmem_limit_bytes=VMEM_LIMIT_BYTES)


def _sigmoid(x):
    return 1.0 / (1.0 + jnp.exp(-x))


def _softplus(x):
    return jnp.maximum(x, 0.0) + jnp.log(1.0 + jnp.exp(-jnp.abs(x)))


def _tile(n, pref):
    if n <= pref:
        return n
    t = pref
    while n % t:
        t //= 2
    return t


def _cond_kernel(c_ref, w_ref, b_ref, o_ref):
    c = c_ref[...]
    s = c * _sigmoid(c)
    r = jnp.dot(s, w_ref[...], precision=HIGHEST, preferred_element_type=F32) + b_ref[...]
    o_ref[...] = r * _sigmoid(r)


def _mod_kernel(cond_ref, w_ref, b_ref, o_ref):
    o_ref[0] = jnp.dot(cond_ref[...], w_ref[0], precision=HIGHEST,
                       preferred_element_type=F32) + b_ref[0]


def _conditioning(c, w_cond, b_cond, w_mod, b_mod):
    bsz, d = c.shape
    rank = w_cond.shape[1]
    depth, _, nmod = w_mod.shape
    rows = SUBLANES
    c8 = jnp.broadcast_to(c[:1], (rows, d))
    cond = pl.pallas_call(
        _cond_kernel,
        out_shape=jax.ShapeDtypeStruct((rows, rank), F32),
        compiler_params=_params(None),
        name="cond_trunk",
    )(c8, w_cond, b_cond.reshape(1, rank))
    tn = _tile(nmod, 4096)
    mod = pl.pallas_call(
        _mod_kernel,
        grid=(depth, nmod // tn),
        in_specs=[
            pl.BlockSpec((rows, rank), lambda i, j: (0, 0)),
            pl.BlockSpec((1, rank, tn), lambda i, j: (i, 0, j)),
            pl.BlockSpec((1, 1, tn), lambda i, j: (i, 0, j)),
        ],
        out_specs=pl.BlockSpec((1, rows, tn), lambda i, j: (i, 0, j)),
        out_shape=jax.ShapeDtypeStruct((depth, rows, nmod), F32),
        compiler_params=_params(("arbitrary", "arbitrary")),
        name="cond_mod",
    )(cond, w_mod, b_mod.reshape(depth, 1, nmod))
    return mod[:, 0, :]


def _rms(y):
    return y * lax.rsqrt(jnp.mean(y * y, axis=-1, keepdims=True) + NORM_EPS)


def _pack_bf16_pairs(v):
    half = v.shape[1] // 2
    bits = lax.bitcast_convert_type(v.astype(BF16).astype(F32), jnp.uint32)
    return (bits[:, :half] >> 16) | bits[:, half:]


def _unpack_bf16_pairs(w):
    lo = lax.bitcast_convert_type(w << 16, F32)
    hi = lax.bitcast_convert_type(w & jnp.uint32(0xFFFF0000), F32)
    return jnp.concatenate([lo, hi], axis=1)


def _row_copy(src_hbm, row, dst, r, sem):
    return pltpu.make_async_copy(src_hbm.at[pl.ds(row, 1)], dst.at[pl.ds(r, 1)], sem)


def _gather_start(src_hbm, idx_ref, base, n, dst, sem):
    def body(r, carry):
        _row_copy(src_hbm, idx_ref[base + r], dst, r, sem).start()
        return carry

    lax.fori_loop(0, n, body, 0, unroll=GATHER_UNROLL)


def _gather_wait(src_hbm, n, dst, sem):
    def body(r, carry):
        _row_copy(src_hbm, 0, dst, r, sem).wait()
        return carry

    lax.fori_loop(0, n, body, 0, unroll=GATHER_UNROLL)


def _glue_kernel(*refs, has_resid, has_next, has_router, gathered):
    it = iter(refs)
    if gathered:
        pos_ref = next(it)
    x_ref = next(it)
    if gathered:
        rw_ref = next(it)
    if has_resid:
        y_ref, ga_ref, gpost_ref = next(it), next(it), next(it)
    if has_next:
        gpre_ref, sc_ref, sh_ref = next(it), next(it), next(it)
    if has_router:
        wr_ref, br_ref = next(it), next(it)
    if has_resid:
        xo_ref = next(it)
    if has_next:
        h_ref = next(it)
    if has_router:
        route_ref = next(it)
    if gathered:
        buf, sem = next(it), next(it)

    x = x_ref[...]
    if has_resid:
        if gathered:
            i = pl.program_id(0)
            tm = x.shape[0]
            nrow = TOP_K * tm
            slot = i % 2

            @pl.when(i == 0)
            def _():
                _gather_start(y_ref, pos_ref, 0, nrow, buf.at[0], sem.at[0])

            @pl.when(i + 1 < pl.num_programs(0))
            def _():
                _gather_start(y_ref, pos_ref, (i + 1) * nrow, nrow, buf.at[1 - slot],
                              sem.at[1 - slot])

            _gather_wait(y_ref, nrow, buf.at[slot], sem.at[slot])
            rw = rw_ref[...]
            y = rw[:, TOP_K:TOP_K + 1] * _unpack_bf16_pairs(buf[slot, 0:tm, :])
            for k in range(1, TOP_K):
                y = y + (rw[:, TOP_K + k:TOP_K + k + 1]
                         * _unpack_bf16_pairs(buf[slot, k * tm:(k + 1) * tm, :]))
        else:
            y = y_ref[...]
        x = x + ga_ref[...] * (_rms(y) * gpost_ref[...])
        xo_ref[...] = x
    if has_next:
        h = (_rms(x) * gpre_ref[...]) * (1.0 + sc_ref[...]) + sh_ref[...]
        if has_router:
            h_ref[...] = _pack_bf16_pairs(h)
        else:
            h_ref[...] = h.astype(h_ref.dtype)
    if has_router:
        logits = jnp.dot(h, wr_ref[...], precision=HIGHEST,
                         preferred_element_type=F32) + br_ref[...]
        lane = lax.broadcasted_iota(jnp.int32, logits.shape, 1).astype(F32)
        work = logits
        vals, idxs = [], []
        for _ in range(TOP_K):
            m = jnp.max(work, axis=1, keepdims=True)
            idx = jnp.min(jnp.where(work == m, lane, float(LANES)), axis=1, keepdims=True)
            vals.append(m)
            idxs.append(idx)
            work = jnp.where(lane == idx, NEG_INF, work)
        exps = [jnp.exp(v - vals[0]) for v in vals]
        denom = exps[0]
        for e in exps[1:]:
            denom = denom + e
        route = jnp.zeros_like(logits)
        for j in range(TOP_K):
            route = jnp.where(lane == float(j), idxs[j], route)
            route = jnp.where(lane == float(TOP_K + j), exps[j] / denom, route)
        route_ref[...] = route


def _glue(x, resid=None, nxt=None, router=None, gather=None, tm=GLUE_ROWS):
    t, d = x.shape
    tm = _tile(t, tm)
    gathered = gather is not None
    row = pl.BlockSpec((tm, d), lambda i, *_: (i, 0))
    vec = pl.BlockSpec((1, d), lambda i, *_: (0, 0))
    lanes = pl.BlockSpec((tm, LANES), lambda i, *_: (i, 0))
    args, in_specs, out_shape, out_specs, scratch = [x], [row], [], [], []
    if gathered:
        args.append(gather[1])
        in_specs.append(lanes)
    if resid is not None:
        y, ga, gpost = resid
        args += [y, ga.reshape(1, d), gpost.reshape(1, d)]
        in_specs += [pl.BlockSpec(memory_space=pl.ANY) if gathered else row, vec, vec]
        out_shape.append(jax.ShapeDtypeStruct((t, d), F32))
        out_specs.append(row)
    if nxt is not None:
        gpre, sc, sh = nxt
        args += [gpre.reshape(1, d), sc.reshape(1, d), sh.reshape(1, d)]
        in_specs += [vec, vec, vec]
        if router is None:
            out_shape.append(jax.ShapeDtypeStruct((t, d), BF16))
            out_specs.append(row)
        else:
            out_shape.append(jax.ShapeDtypeStruct((t, d // 2), jnp.uint32))
            out_specs.append(pl.BlockSpec((tm, d // 2), lambda i, *_: (i, 0)))
    if router is not None:
        wr, br = router
        args += [wr, br]
        in_specs += [pl.BlockSpec((d, LANES), lambda i, *_: (0, 0)),
                     pl.BlockSpec((1, LANES), lambda i, *_: (0, 0))]
        out_shape.append(jax.ShapeDtypeStruct((t, LANES), F32))
        out_specs.append(lanes)
    if gathered:
        args = [gather[0]] + args
        scratch = [pltpu.VMEM((2, TOP_K * tm, d // 2), jnp.uint32),
                   pltpu.SemaphoreType.DMA((2,))]
    outs = pl.pallas_call(
        functools.partial(_glue_kernel, has_resid=resid is not None, has_next=nxt is not None,
                          has_router=router is not None, gathered=gathered),
        grid_spec=pltpu.PrefetchScalarGridSpec(
            num_scalar_prefetch=1 if gathered else 0,
            grid=(t // tm,),
            in_specs=in_specs,
            out_specs=out_specs,
            scratch_shapes=scratch,
        ),
        out_shape=out_shape,
        compiler_params=_params(("arbitrary",)),
        name="glue_gather" if gathered else "glue",
    )(*args)
    return outs


def _mm_kernel(a_ref, w_ref, o_ref, *scratch, nk, scale):
    part = jnp.dot(a_ref[...], w_ref[0].astype(BF16), preferred_element_type=F32)
    if nk == 1:
        if scale != 1.0:
            part = part * scale
        o_ref[...] = part.astype(o_ref.dtype)
        return
    acc_ref = scratch[0] if scratch else o_ref
    k = pl.program_id(2)

    @pl.when(k == 0)
    def _():
        acc_ref[...] = part

    @pl.when(k > 0)
    def _():
        acc_ref[...] += part

    if scratch or scale != 1.0:
        @pl.when(k == nk - 1)
        def _():
            r = acc_ref[...]
            if scale != 1.0:
                r = r * scale
            o_ref[...] = r.astype(o_ref.dtype)


def _matmul(a, w, layer, out_dtype, col0=0, ncols=None, scale=1.0, tm=1024, tn=512, tk=4096):
    m, k = a.shape
    n = w.shape[2] - col0 if ncols is None else ncols
    tm, tn, tk = _tile(m, tm), _tile(n, tn), _tile(k, tk)
    assert col0 % tn == 0 and n % tn == 0
    nk = k // tk
    jb = col0 // tn
    scratch = []
    if nk > 1 and out_dtype != F32:
        scratch = [pltpu.VMEM((tm, tn), F32)]
    return pl.pallas_call(
        functools.partial(_mm_kernel, nk=nk, scale=scale),
        grid=(m // tm, n // tn, nk),
        in_specs=[pl.BlockSpec((tm, tk), lambda i, j, kk: (i, kk)),
                  pl.BlockSpec((1, tk, tn), lambda i, j, kk: (layer, kk, jb + j))],
        out_specs=pl.BlockSpec((tm, tn), lambda i, j, kk: (i, j)),
        out_shape=jax.ShapeDtypeStruct((m, n), out_dtype),
        scratch_shapes=scratch,
        compiler_params=_params(("arbitrary", "arbitrary", "arbitrary")),
        name="matmul",
    )(a, w)


def _attn_kernel(q_ref, k_ref, v_ref, o_ref, acc_ref, carry_ref, *, tq, tk):
    i = pl.program_id(1)
    r2 = lax.broadcasted_iota(jnp.int32, (tk, tk), 0)
    c2 = lax.broadcasted_iota(jnp.int32, (tk, tk), 1)
    later = jnp.where(r2 > c2, 1.0, 0.0).astype(BF16)

    acc_ref[...] = jnp.zeros(acc_ref.shape, F32)
    carry_ref[...] = jnp.zeros(carry_ref.shape, F32)

    def span(k_start, nsub, r0, diag):
        kk = k_ref[pl.ds(k_start, nsub * tk), :]
        vv = v_ref[pl.ds(k_start, nsub * tk), :]
        n = lax.dot_general(q_ref[r0:, :], kk, (((1,), (1,)), ((), ())),
                            preferred_element_type=F32)
        neg_abs = lax.bitcast_convert_type(
            lax.bitcast_convert_type(n, jnp.uint32) | jnp.uint32(0x80000000), F32)
        lk = jnp.minimum(n, 0.0) - jnp.log(1.0 + jnp.exp2(neg_abs)) * LOG2E
        if diag:
            row = lax.broadcasted_iota(jnp.int32, n.shape, 0)
            col = lax.broadcasted_iota(jnp.int32, n.shape, 1)
            vis = col < row
            lk = jnp.where(vis, lk, 0.0)
        carry = carry_ref[r0:, :]
        ws = [None] * nsub
        for s in range(nsub - 1, -1, -1):
            lk_s = lk[:, s * tk:(s + 1) * tk]
            la = jnp.dot(lk_s.astype(BF16), later, preferred_element_type=F32) + carry
            ws[s] = jnp.exp2((la + lk_s) - n[:, s * tk:(s + 1) * tk])
            carry = la[:, 0:1] + lk_s[:, 0:1]
        w = ws[0] if nsub == 1 else jnp.concatenate(ws, axis=1)
        if diag:
            w = jnp.where(vis, w, 0.0)
        acc_ref[r0:, :] += jnp.dot(w.astype(BF16), vv, preferred_element_type=F32)
        carry_ref[r0:, :] = carry

    q0 = i * tq
    for d in range(tq // tk - 1, -1, -1):
        span(pl.multiple_of(q0 + d * tk, tk), 1, d * tk, True)

    nsub = math.gcd(tq // tk, 2)
    nstep = q0 // (nsub * tk)

    def more(state):
        t, top = state
        return jnp.logical_and(t < nstep, top > F32_UNDERFLOW_LOG2)

    def body(state):
        t, _ = state
        span(pl.multiple_of((nstep - 1 - t) * (nsub * tk), nsub * tk), nsub, 0, False)
        return t + 1, jnp.max(carry_ref[...])

    lax.while_loop(more, body, (jnp.int32(0), jnp.max(carry_ref[...])))
    o_ref[...] = acc_ref[...].astype(o_ref.dtype)


def _attention(q, kv, tq=1024, tk=256):
    l, d = q.shape
    nh = d // SB_HEAD_DIM
    tq = _tile(l, tq)
    tk = _tile(tq, tk)
    return pl.pallas_call(
        functools.partial(_attn_kernel, tq=tq, tk=tk),
        grid=(nh, l // tq),
        in_specs=[
            pl.BlockSpec((tq, SB_HEAD_DIM), lambda h, i: (i, h)),
            pl.BlockSpec((l, SB_HEAD_DIM), lambda h, i: (0, h)),
            pl.BlockSpec((l, SB_HEAD_DIM), lambda h, i: (0, nh + h)),
        ],
        out_specs=pl.BlockSpec((tq, SB_HEAD_DIM), lambda h, i: (i, h)),
        out_shape=jax.ShapeDtypeStruct((l, d), BF16),
        scratch_shapes=[pltpu.VMEM((tq, SB_HEAD_DIM), F32), pltpu.VMEM((tq, 1), F32)],
        compiler_params=_params(("arbitrary", "arbitrary")),
        name="sb_attention",
    )(q, kv, kv)


def _conv_kernel(u_ref, w_ref, b_ref, o_ref, ext_ref, *, tl):
    halo = SUBLANES

    @pl.when(pl.program_id(1) == 0)
    def _():
        ext_ref[0:halo, :] = jnp.zeros((halo, ext_ref.shape[1]), F32)

    u = u_ref[...]
    ext_ref[halo:halo + tl, :] = u
    acc = b_ref[...] + w_ref[SSD_CONV - 1:SSD_CONV, :] * u
    for k in range(SSD_CONV - 1):
        off = halo - (SSD_CONV - 1) + k
        acc = acc + w_ref[k:k + 1, :] * ext_ref[off:off + tl, :]
    o_ref[...] = acc * _sigmoid(acc)
    ext_ref[0:halo, :] = u[tl - halo:tl, :]


def _conv_silu(u, w, b, tl=512, tc=1024):
    l, c = u.shape
    tl, tc = _tile(l, tl), _tile(c, tc)
    return pl.pallas_call(
        functools.partial(_conv_kernel, tl=tl),
        grid=(c // tc, l // tl),
        in_specs=[
            pl.BlockSpec((tl, tc), lambda j, i: (i, j)),
            pl.BlockSpec((SSD_CONV, tc), lambda j, i: (0, j)),
            pl.BlockSpec((1, tc), lambda j, i: (0, j)),
        ],
        out_specs=pl.BlockSpec((tl, tc), lambda j, i: (i, j)),
        out_shape=jax.ShapeDtypeStruct((l, c), F32),
        scratch_shapes=[pltpu.VMEM((tl + SUBLANES, tc), F32)],
        compiler_params=_params(("arbitrary", "arbitrary")),
        name="ssd_conv",
    )(u, w, b.reshape(1, c))


def _split3(x):
    p1 = x.astype(BF16)
    r1 = x - p1.astype(F32)
    p2 = r1.astype(BF16)
    p3 = (r1 - p2.astype(F32)).astype(BF16)
    return p1, p2, p3


def _dtprep_kernel(dt_ref, bias_ref, alog_ref, dtr_ref, acsr_ref, acsc_ref, *, hpg):
    dt = _softplus(dt_ref[...] + bias_ref[...])
    adt = dt * (-jnp.exp(alog_ref[...]))
    n = dt.shape[0]
    r = lax.broadcasted_iota(jnp.int32, (n, n), 0)
    c = lax.broadcasted_iota(jnp.int32, (n, n), 1)
    upto = jnp.where(c <= r, 1.0, 0.0).astype(BF16)
    acs = sum(jnp.dot(upto, p, preferred_element_type=F32) for p in _split3(adt))
    dtr_ref[...] = dt.T
    acsr_ref[...] = acs.T
    for g in range(acsc_ref.shape[0]):
        acsc_ref[g] = acs[:, g * hpg:(g + 1) * hpg]


def _dtprep(dt_raw, dt_bias, a_log):
    l, nh = dt_raw.shape
    hpg = nh // SSD_GROUPS
    ch = SSD_CHUNK
    return pl.pallas_call(
        functools.partial(_dtprep_kernel, hpg=hpg),
        grid=(l // ch,),
        in_specs=[
            pl.BlockSpec((ch, nh), lambda i: (i, 0)),
            pl.BlockSpec((1, nh), lambda i: (0, 0)),
            pl.BlockSpec((1, nh), lambda i: (0, 0)),
        ],
        out_specs=[
            pl.BlockSpec((nh, ch), lambda i: (0, i)),
            pl.BlockSpec((nh, ch), lambda i: (0, i)),
            pl.BlockSpec((SSD_GROUPS, ch, hpg), lambda i: (0, i, 0)),
        ],
        out_shape=[
            jax.ShapeDtypeStruct((nh, l), F32),
            jax.ShapeDtypeStruct((nh, l), F32),
            jax.ShapeDtypeStruct((SSD_GROUPS, l, hpg), F32),
        ],
        compiler_params=_params(("arbitrary",)),
        name="ssd_dtprep",
    )(dt_raw, dt_bias.reshape(1, nh), a_log.reshape(1, nh))


def _ssd_kernel(xs_ref, b_ref, c_ref, dtr_ref, acsr_ref, acsc_ref, z_ref, dexp_ref, nw_ref,
                o_ref, st_ref, y_ref, *, npairs):
    ch = SSD_CHUNK
    hp = SSD_HEAD_DIM

    @pl.when(pl.program_id(1) == 0)
    def _():
        st_ref[...] = jnp.zeros(st_ref.shape, F32)

    bm = b_ref[...]
    cm = c_ref[...]
    cb = lax.dot_general(cm.astype(BF16), bm.astype(BF16), (((1,), (1,)), ((), ())),
                         preferred_element_type=F32)
    bt = bm.T
    row = lax.broadcasted_iota(jnp.int32, (ch, ch), 0)
    col = lax.broadcasted_iota(jnp.int32, (ch, ch), 1)
    causal = col <= row
    first = lax.broadcasted_iota(jnp.int32, (ch, 2 * hp), 1) < hp
    dtr = dtr_ref[...]
    acsr = acsr_ref[...]
    acsc = acsc_ref[0]

    for pr in range(npairs):
        sl = slice(pr * 2 * hp, (pr + 1) * 2 * hp)
        xs_b = xs_ref[:, sl].astype(BF16)
        prev = st_ref[pr]
        rhs = jnp.concatenate([xs_b, prev.astype(BF16)], axis=0)
        outs, news = [], []
        for hh in (2 * pr, 2 * pr + 1):
            a_col = jnp.broadcast_to(acsc[:, hh:hh + 1], (ch, ch))
            a_row = acsr[hh:hh + 1, :]
            dt_row = dtr[hh:hh + 1, :]
            lmat = jnp.exp(jnp.where(causal, a_col - a_row, NEG_INF))
            scores = cb * lmat * dt_row
            c_dec = cm * jnp.exp(a_col)
            lhs = jnp.concatenate([scores.astype(BF16), c_dec.astype(BF16)], axis=1)
            outs.append(jnp.dot(lhs, rhs, preferred_element_type=F32))
            a_last = a_row[:, ch - 1:ch]
            b_dec = (bt * (dt_row * jnp.exp(a_last - a_row))).astype(BF16)
            news.append(jnp.exp(a_last) * prev
                        + jnp.dot(b_dec, xs_b, preferred_element_type=F32))
        y_ref[:, sl] = jnp.where(first, outs[0], outs[1])
        st_ref[pr] = jnp.where(first, news[0], news[1])

    y = y_ref[...] + dexp_ref[...] * xs_ref[...]
    z = z_ref[...]
    yg = y * (z * _sigmoid(z))
    o_ref[...] = (_rms(yg) * nw_ref[...]).astype(o_ref.dtype)


def _ssd_scan(xbc, z, dtr, acsr, acsc, d_exp, norm_w):
    l, d_inner = z.shape
    gw = d_inner // SSD_GROUPS
    npairs = gw // (2 * SSD_HEAD_DIM)
    hpg = gw // SSD_HEAD_DIM
    ch = SSD_CHUNK
    nb = d_inner // SSD_STATE
    return pl.pallas_call(
        functools.partial(_ssd_kernel, npairs=npairs),
        grid=(SSD_GROUPS, l // ch),
        in_specs=[
            pl.BlockSpec((ch, gw), lambda g, c: (c, g)),
            pl.BlockSpec((ch, SSD_STATE), lambda g, c: (c, nb + g)),
            pl.BlockSpec((ch, SSD_STATE), lambda g, c: (c, nb + SSD_GROUPS + g)),
            pl.BlockSpec((hpg, ch), lambda g, c: (g, c)),
            pl.BlockSpec((hpg, ch), lambda g, c: (g, c)),
            pl.BlockSpec((1, ch, hpg), lambda g, c: (g, c, 0)),
            pl.BlockSpec((ch, gw), lambda g, c: (c, g)),
            pl.BlockSpec((1, gw), lambda g, c: (0, g)),
            pl.BlockSpec((1, gw), lambda g, c: (0, g)),
        ],
        out_specs=pl.BlockSpec((ch, gw), lambda g, c: (c, g)),
        out_shape=jax.ShapeDtypeStruct((l, d_inner), BF16),
        scratch_shapes=[pltpu.VMEM((npairs, SSD_STATE, 2 * SSD_HEAD_DIM), F32),
                        pltpu.VMEM((ch, gw), F32)],
        compiler_params=_params(("arbitrary", "arbitrary")),
        name="ssd_scan",
    )(xbc, xbc, xbc, dtr, acsr, acsc, z, d_exp, norm_w.reshape(1, d_inner))


def _deinterleave_kernel(w_ref, p_ref, o_ref):
    o_ref[0] = jnp.dot(w_ref[0, 0].astype(BF16), p_ref[...],
                       preferred_element_type=F32).astype(BF16)


def _deinterleave_cast(w_gu, layer, tr=1024):
    _, ne, d, f2 = w_gu.shape
    tr = _tile(d, tr)
    src = jnp.arange(f2)
    dst = jnp.where(src % 2 == 0, src // 2, f2 // 2 + src // 2)
    perm = (dst[:, None] == jnp.arange(f2)[None, :]).astype(BF16)
    return pl.pallas_call(
        _deinterleave_kernel,
        grid=(ne, d // tr),
        in_specs=[pl.BlockSpec((1, 1, tr, f2), lambda e, i: (layer, e, i, 0)),
                  pl.BlockSpec((f2, f2), lambda e, i: (0, 0))],
        out_specs=pl.BlockSpec((1, tr, f2), lambda e, i: (e, i, 0)),
        out_shape=jax.ShapeDtypeStruct((ne, d, f2), BF16),
        compiler_params=_params(("arbitrary", "arbitrary")),
        name="moe_deinterleave",
    )(w_gu, perm)


def _lookup(table, idx):
    hit = idx[:, None] == jnp.arange(table.shape[0], dtype=jnp.int32)[None, :]
    return jnp.sum(jnp.where(hit, table[None, :], 0), axis=1)


def _route_tables(route, ne, tr, tm):
    t = route.shape[0]
    a = t * TOP_K
    eidx = route[:, :TOP_K].astype(jnp.int32).reshape(a)
    order = jnp.argsort(eidx, stable=True).astype(jnp.int32)
    rank = jnp.argsort(order).astype(jnp.int32)
    counts = jnp.sum(eidx[:, None] == jnp.arange(ne, dtype=jnp.int32)[None, :], axis=0,
                     dtype=jnp.int32)
    starts = jnp.cumsum(counts) - counts
    padded = (counts + tr - 1) // tr * tr
    pends = jnp.cumsum(padded)
    pstarts = pends - padded
    pos = rank + _lookup(pstarts - starts, eidx)
    n_tiles = (a + ne * tr) // tr
    used = (pends[-1] // tr).astype(jnp.int32)
    tile0 = jnp.arange(n_tiles, dtype=jnp.int32) * tr
    tile0_used = jnp.minimum(tile0, (used - 1) * tr)
    tile_e = jnp.sum(pends[None, :] <= tile0_used[:, None], axis=1, dtype=jnp.int32)
    src_base = jnp.clip(_lookup(starts - pstarts, tile_e) + tile0_used, 0, a)
    order_tok = jnp.concatenate([order // TOP_K, jnp.zeros((tr,), jnp.int32)])
    pos_blk = pos.reshape(t // tm, tm, TOP_K).transpose(0, 2, 1).reshape(a)
    return order_tok, tile_e, src_base, used.reshape(1), pos_blk


def _moe_kernel(tile_e_ref, base_ref, used_ref, tok_ref, h_hbm, wgl_ref, wd_ref, bg_ref, bl_ref,
                bd_ref, o_ref, buf, sem, *, tr):
    t = pl.program_id(0)
    used = used_ref[0]
    slot = t % 2
    f = bg_ref.shape[-1]

    @pl.when(t == 0)
    def _():
        _gather_start(h_hbm, tok_ref, base_ref[0], tr, buf.at[0], sem.at[0])

    @pl.when(t + 1 < used)
    def _():
        _gather_start(h_hbm, tok_ref, base_ref[t + 1], tr, buf.at[1 - slot], sem.at[1 - slot])

    @pl.when(t < used)
    def _():
        _gather_wait(h_hbm, tr, buf.at[slot], sem.at[slot])
        gl = jnp.dot(_unpack_bf16_pairs(buf[slot]).astype(BF16), wgl_ref[0],
                     preferred_element_type=F32)
        xg = jnp.minimum(gl[:, :f] + bg_ref[0], SWIGLU_LIMIT)
        xl = jnp.clip(gl[:, f:] + bl_ref[0], -SWIGLU_LIMIT, SWIGLU_LIMIT)
        act = xg * _sigmoid(SWIGLU_ALPHA * xg) * (xl + 1.0)
        o_ref[...] = _pack_bf16_pairs(jnp.dot(act.astype(BF16), wd_ref[0, 0].astype(BF16),
                                              preferred_element_type=F32) + bd_ref[0])

    @pl.when(t >= used)
    def _():
        o_ref[...] = jnp.zeros(o_ref.shape, o_ref.dtype)


def _moe_sparse(h, w_gl, w_down, layer, b_g, b_l, b_down, order_tok, tile_e, src_base, used, tr):
    ne, d, f2 = w_gl.shape
    f = f2 // 2
    n_tiles = tile_e.shape[0]
    return pl.pallas_call(
        functools.partial(_moe_kernel, tr=tr),
        grid_spec=pltpu.PrefetchScalarGridSpec(
            num_scalar_prefetch=4,
            grid=(n_tiles,),
            in_specs=[
                pl.BlockSpec(memory_space=pl.ANY),
                pl.BlockSpec((1, d, f2), lambda i, te, *_: (te[i], 0, 0)),
                pl.BlockSpec((1, 1, f, d), lambda i, te, *_: (layer, te[i], 0, 0)),
                pl.BlockSpec((1, 1, f), lambda i, te, *_: (te[i], 0, 0)),
                pl.BlockSpec((1, 1, f), lambda i, te, *_: (te[i], 0, 0)),
                pl.BlockSpec((1, 1, d), lambda i, te, *_: (te[i], 0, 0)),
            ],
            out_specs=pl.BlockSpec((tr, d // 2), lambda i, *_: (i, 0)),
            scratch_shapes=[pltpu.VMEM((2, tr, d // 2), jnp.uint32),
                            pltpu.SemaphoreType.DMA((2,))],
        ),
        out_shape=jax.ShapeDtypeStruct((n_tiles * tr, d // 2), jnp.uint32),
        compiler_params=_params(("arbitrary",)),
        name="moe_experts",
    )(tile_e, src_base, used, order_tok, h, w_gl, w_down, b_g, b_l, b_down)


def kernel(x, c, w_cond, b_cond, w_mod, b_mod, g_mix_pre, g_mix_post, g_ffn_pre, g_ffn_post, attn_w_in, attn_w_out, ssd_w_in, ssd_conv_w, ssd_conv_b, ssd_dt_bias, ssd_a_log, ssd_d, ssd_norm_w, ssd_w_out, moe_w_router, moe_b_router, moe_w_gu, moe_b_gu, moe_w_down, moe_b_down):
    bsz, seq, d = x.shape
    assert bsz == 1, "kernel is written for a single sequence"
    depth = w_mod.shape[0]
    ne = moe_w_router.shape[-1]
    ff = moe_w_down.shape[2]
    d_inner = ssd_w_out.shape[1]
    conv_dim = ssd_conv_w.shape[-1]

    mod = _conditioning(c, w_cond, b_cond, w_mod, b_mod)

    def mods(i):
        return [mod[i, k * d:(k + 1) * d] for k in range(N_MOD)]

    xt = x.reshape(seq, d)
    tm = _tile(seq, GLUE_ROWS)
    sh_m, sc_m, ga_m, sh_f, sc_f, ga_f = mods(0)
    (h,) = _glue(xt, nxt=(g_mix_pre[0], sc_m, sh_m), tm=tm)

    for i in range(depth):
        j = i // 2
        if i % 2 == 0:
            q = _matmul(h, attn_w_in, j, BF16, col0=0, ncols=d,
                        scale=-LOG2E * SB_HEAD_DIM ** -0.5)
            kv = _matmul(h, attn_w_in, j, BF16, col0=d, ncols=2 * d)
            o = _attention(q, kv)
            y = _matmul(o, attn_w_out, j, F32)
        else:
            z = _matmul(h, ssd_w_in, j, F32, col0=0, ncols=d_inner)
            xbc = _matmul(h, ssd_w_in, j, F32, col0=d_inner, ncols=conv_dim)
            dt_raw = _matmul(h, ssd_w_in, j, F32, col0=d_inner + conv_dim)
            xbc = _conv_silu(xbc, ssd_conv_w[j], ssd_conv_b[j])
            dtr, acsr, acsc = _dtprep(dt_raw, ssd_dt_bias[j], ssd_a_log[j])
            d_exp = jnp.repeat(ssd_d[j], SSD_HEAD_DIM).reshape(1, d_inner)
            yn = _ssd_scan(xbc, z, dtr, acsr, acsc, d_exp, ssd_norm_w[j])
            y = _matmul(yn, ssd_w_out, j, F32)

        wr = jnp.pad(moe_w_router[i], ((0, 0), (0, LANES - ne)))
        br = jnp.pad(moe_b_router[i], (0, LANES - ne), constant_values=NEG_INF).reshape(1, LANES)
        xt, h, route = _glue(xt, resid=(y, ga_m, g_mix_post[i]),
                             nxt=(g_ffn_pre[i], sc_f, sh_f), router=(wr, br), tm=tm)

        order_tok, tile_e, src_base, used, pos_blk = _route_tables(route, ne, MOE_TILE_ROWS, tm)
        w_gl = _deinterleave_cast(moe_w_gu, i)
        b_g = moe_b_gu[i][:, 0::2].reshape(ne, 1, ff)
        b_l = moe_b_gu[i][:, 1::2].reshape(ne, 1, ff)
        y_rows = _moe_sparse(h, w_gl, moe_w_down, i, b_g, b_l, moe_b_down[i].reshape(ne, 1, d),
                             order_tok, tile_e, src_base, used, MOE_TILE_ROWS)

        ga_prev = ga_f
        if i + 1 < depth:
            sh_m, sc_m, ga_m, sh_f, sc_f, ga_f = mods(i + 1)
            xt, h = _glue(xt, resid=(y_rows, ga_prev, g_ffn_post[i]),
                          nxt=(g_mix_pre[i + 1], sc_m, sh_m), gather=(pos_blk, route), tm=tm)
        else:
            (xt,) = _glue(xt, resid=(y_rows, ga_prev, g_ffn_post[i]), gather=(pos_blk, route),
                          tm=tm)

    return xt.reshape(bsz, seq, d)
```

```python
import functools
import math

import jax
import jax.numpy as jnp
from jax import lax
from jax.experimental import pallas as pl
from jax.experimental.pallas import tpu as pltpu

SB_HEAD_DIM = 128
SSD_HEAD_DIM = 64
SSD_GROUPS = 8
SSD_STATE = 128
SSD_CONV = 4
SSD_CHUNK = 128
TOP_K = 4
SWIGLU_ALPHA = 1.702
SWIGLU_LIMIT = 7.0
N_MOD = 6
NORM_EPS = 1e-6

LANES = 128
SUBLANES = 8
VMEM_LIMIT_BYTES = 56 * 1024 * 1024

GLUE_ROWS = 128
MOE_TILE_ROWS = 256
GATHER_UNROLL = 8

F32 = jnp.float32
BF16 = jnp.bfloat16
HIGHEST = lax.Precision.HIGHEST
NEG_INF = float("-inf")
LOG2E = math.log2(math.e)
F32_UNDERFLOW_LOG2 = -160.0


def _params(sem):
    return pltpu.CompilerParams(dimension_semantics=sem, vmem_limit_bytes=VMEM_LIMIT_BYTES)


def _sigmoid(x):
    return 1.0 / (1.0 + jnp.exp(-x))


def _softplus(x):
    return jnp.maximum(x, 0.0) + jnp.log(1.0 + jnp.exp(-jnp.abs(x)))


def _tile(n, pref):
    if n <= pref:
        return n
    t = pref
    while n % t:
        t //= 2
    return t


def _cond_kernel(c_ref, w_ref, b_ref, o_ref):
    c = c_ref[...]
    s = c * _sigmoid(c)
    r = jnp.dot(s, w_ref[...], precision=HIGHEST, preferred_element_type=F32) + b_ref[...]
    o_ref[...] = r * _sigmoid(r)


def _mod_kernel(cond_ref, w_ref, b_ref, o_ref):
    o_ref[0] = jnp.dot(cond_ref[...], w_ref[0], precision=HIGHEST,
                       preferred_element_type=F32) + b_ref[0]


def _conditioning(c, w_cond, b_cond, w_mod, b_mod):
    bsz, d = c.shape
    rank = w_cond.shape[1]
    depth, _, nmod = w_mod.shape
    rows = SUBLANES
    c8 = jnp.broadcast_to(c[:1], (rows, d))
    cond = pl.pallas_call(
        _cond_kernel,
        out_shape=jax.ShapeDtypeStruct((rows, rank), F32),
        compiler_params=_params(None),
        name="cond_trunk",
    )(c8, w_cond, b_cond.reshape(1, rank))
    tn = _tile(nmod, 4096)
    mod = pl.pallas_call(
        _mod_kernel,
        grid=(depth, nmod // tn),
        in_specs=[
            pl.BlockSpec((rows, rank), lambda i, j: (0, 0)),
            pl.BlockSpec((1, rank, tn), lambda i, j: (i, 0, j)),
            pl.BlockSpec((1, 1, tn), lambda i, j: (i, 0, j)),
        ],
        out_specs=pl.BlockSpec((1, rows, tn), lambda i, j: (i, 0, j)),
        out_shape=jax.ShapeDtypeStruct((depth, rows, nmod), F32),
        compiler_params=_params(("arbitrary", "arbitrary")),
        name="cond_mod",
    )(cond, w_mod, b_mod.reshape(depth, 1, nmod))
    return mod[:, 0, :]


def _rms(y):
    return y * lax.rsqrt(jnp.mean(y * y, axis=-1, keepdims=True) + NORM_EPS)


def _pack_bf16_pairs(v):
    half = v.shape[1] // 2
    bits = lax.bitcast_convert_type(v.astype(BF16).astype(F32), jnp.uint32)
    return (bits[:, :half] >> 16) | bits[:, half:]


def _unpack_bf16_pairs(w):
    lo = lax.bitcast_convert_type(w << 16, F32)
    hi = lax.bitcast_convert_type(w & jnp.uint32(0xFFFF0000), F32)
    return jnp.concatenate([lo, hi], axis=1)


def _row_copy(src_hbm, row, dst, r, sem):
    return pltpu.make_async_copy(src_hbm.at[pl.ds(row, 1)], dst.at[pl.ds(r, 1)], sem)


def _gather_start(src_hbm, idx_ref, base, n, dst, sem):
    def body(r, carry):
        _row_copy(src_hbm, idx_ref[base + r], dst, r, sem).start()
        return carry

    lax.fori_loop(0, n, body, 0, unroll=GATHER_UNROLL)


def _gather_wait(src_hbm, n, dst, sem):
    def body(r, carry):
        _row_copy(src_hbm, 0, dst, r, sem).wait()
        return carry

    lax.fori_loop(0, n, body, 0, unroll=GATHER_UNROLL)


def _glue_kernel(*refs, has_resid, has_next, has_router, gathered):
    it = iter(refs)
    if gathered:
        pos_ref = next(it)
    x_ref = next(it)
    if gathered:
        rw_ref = next(it)
    if has_resid:
        y_ref, ga_ref, gpost_ref = next(it), next(it), next(it)
    if has_next:
        gpre_ref, sc_ref, sh_ref = next(it), next(it), next(it)
    if has_router:
        wr_ref, br_ref = next(it), next(it)
    if has_resid:
        xo_ref = next(it)
    if has_next:
        h_ref = next(it)
    if has_router:
        route_ref = next(it)
    if gathered:
        buf, sem = next(it), next(it)

    x = x_ref[...]
    if has_resid:
        if gathered:
            i = pl.program_id(0)
            tm = x.shape[0]
            nrow = TOP_K * tm
            slot = i % 2

            @pl.when(i == 0)
            def _():
                _gather_start(y_ref, pos_ref, 0, nrow, buf.at[0], sem.at[0])

            _gather_wait(y_ref, nrow, buf.at[slot], sem.at[slot])
            last = pl.num_programs(0) - 1
            nxt_base = jnp.minimum(i + 1, last) * nrow
            for r in range(nrow):
                _row_copy(y_ref, pos_ref[nxt_base + r], buf.at[1 - slot], r,
                          sem.at[1 - slot]).start()
            rw = rw_ref[...]
            y = rw[:, TOP_K:TOP_K + 1] * _unpack_bf16_pairs(buf[slot, 0:tm, :])
            for k in range(1, TOP_K):
                y = y + (rw[:, TOP_K + k:TOP_K + k + 1]
                         * _unpack_bf16_pairs(buf[slot, k * tm:(k + 1) * tm, :]))
        else:
            y = y_ref[...]
        x = x + ga_ref[...] * (_rms(y) * gpost_ref[...])
        xo_ref[...] = x
    if has_next:
        h = (_rms(x) * gpre_ref[...]) * (1.0 + sc_ref[...]) + sh_ref[...]
        if has_router:
            h_ref[...] = _pack_bf16_pairs(h)
        else:
            h_ref[...] = h.astype(h_ref.dtype)
    if has_router:
        h_hi = h.astype(BF16)
        h_lo = (h - h_hi.astype(F32)).astype(BF16)
        w_hi, w_lo = wr_ref[0], wr_ref[1]
        logits = (jnp.dot(h_hi, w_hi, preferred_element_type=F32)
                  + (jnp.dot(h_lo, w_hi, preferred_element_type=F32)
                     + jnp.dot(h_hi, w_lo, preferred_element_type=F32))) + br_ref[...]
        lane = lax.broadcasted_iota(jnp.int32, logits.shape, 1).astype(F32)
        work = logits
        vals, idxs = [], []
        for _ in range(TOP_K):
            m = jnp.max(work, axis=1, keepdims=True)
            idx = jnp.min(jnp.where(work == m, lane, float(LANES)), axis=1, keepdims=True)
            vals.append(m)
            idxs.append(idx)
            work = jnp.where(lane == idx, NEG_INF, work)
        exps = [jnp.exp(v - vals[0]) for v in vals]
        denom = exps[0]
        for e in exps[1:]:
            denom = denom + e
        route = jnp.zeros_like(logits)
        for j in range(TOP_K):
            route = jnp.where(lane == float(j), idxs[j], route)
            route = jnp.where(lane == float(TOP_K + j), exps[j] / denom, route)
        route_ref[...] = route
    if gathered:
        @pl.when(i == last)
        def _():
            _gather_wait(y_ref, nrow, buf.at[1 - slot], sem.at[1 - slot])


def _glue(x, resid=None, nxt=None, router=None, gather=None, tm=GLUE_ROWS):
    t, d = x.shape
    tm = _tile(t, tm)
    gathered = gather is not None
    row = pl.BlockSpec((tm, d), lambda i, *_: (i, 0))
    vec = pl.BlockSpec((1, d), lambda i, *_: (0, 0))
    lanes = pl.BlockSpec((tm, LANES), lambda i, *_: (i, 0))
    args, in_specs, out_shape, out_specs, scratch = [x], [row], [], [], []
    if gathered:
        args.append(gather[1])
        in_specs.append(lanes)
    if resid is not None:
        y, ga, gpost = resid
        args += [y, ga.reshape(1, d), gpost.reshape(1, d)]
        in_specs += [pl.BlockSpec(memory_space=pl.ANY) if gathered else row, vec, vec]
        out_shape.append(jax.ShapeDtypeStruct((t, d), F32))
        out_specs.append(row)
    if nxt is not None:
        gpre, sc, sh = nxt
        args += [gpre.reshape(1, d), sc.reshape(1, d), sh.reshape(1, d)]
        in_specs += [vec, vec, vec]
        if router is None:
            out_shape.append(jax.ShapeDtypeStruct((t, d), BF16))
            out_specs.append(row)
        else:
            out_shape.append(jax.ShapeDtypeStruct((t, d // 2), jnp.uint32))
            out_specs.append(pl.BlockSpec((tm, d // 2), lambda i, *_: (i, 0)))
    if router is not None:
        wr, br = router
        args += [wr, br]
        in_specs += [pl.BlockSpec((2, d, LANES), lambda i, *_: (0, 0, 0)),
                     pl.BlockSpec((1, LANES), lambda i, *_: (0, 0))]
        out_shape.append(jax.ShapeDtypeStruct((t, LANES), F32))
        out_specs.append(lanes)
    if gathered:
        args = [gather[0]] + args
        scratch = [pltpu.VMEM((2, TOP_K * tm, d // 2), jnp.uint32),
                   pltpu.SemaphoreType.DMA((2,))]
    outs = pl.pallas_call(
        functools.partial(_glue_kernel, has_resid=resid is not None, has_next=nxt is not None,
                          has_router=router is not None, gathered=gathered),
        grid_spec=pltpu.PrefetchScalarGridSpec(
            num_scalar_prefetch=1 if gathered else 0,
            grid=(t // tm,),
            in_specs=in_specs,
            out_specs=out_specs,
            scratch_shapes=scratch,
        ),
        out_shape=out_shape,
        compiler_params=_params(("arbitrary",)),
        name="glue_gather" if gathered else "glue",
    )(*args)
    return outs


def _mm_kernel(a_ref, w_ref, o_ref, *scratch, nk, scale):
    part = jnp.dot(a_ref[...], w_ref[0].astype(BF16), preferred_element_type=F32)
    if nk == 1:
        if scale != 1.0:
            part = part * scale
        o_ref[...] = part.astype(o_ref.dtype)
        return
    acc_ref = scratch[0] if scratch else o_ref
    k = pl.program_id(2)

    @pl.when(k == 0)
    def _():
        acc_ref[...] = part

    @pl.when(k > 0)
    def _():
        acc_ref[...] += part

    if scratch or scale != 1.0:
        @pl.when(k == nk - 1)
        def _():
            r = acc_ref[...]
            if scale != 1.0:
                r = r * scale
            o_ref[...] = r.astype(o_ref.dtype)


def _matmul(a, w, layer, out_dtype, col0=0, ncols=None, scale=1.0, tm=1024, tn=512, tk=4096):
    m, k = a.shape
    n = w.shape[2] - col0 if ncols is None else ncols
    tm, tn, tk = _tile(m, tm), _tile(n, tn), _tile(k, tk)
    assert col0 % tn == 0 and n % tn == 0
    nk = k // tk
    jb = col0 // tn
    scratch = []
    if nk > 1 and out_dtype != F32:
        scratch = [pltpu.VMEM((tm, tn), F32)]
    return pl.pallas_call(
        functools.partial(_mm_kernel, nk=nk, scale=scale),
        grid=(m // tm, n // tn, nk),
        in_specs=[pl.BlockSpec((tm, tk), lambda i, j, kk: (i, kk)),
                  pl.BlockSpec((1, tk, tn), lambda i, j, kk: (layer, kk, jb + j))],
        out_specs=pl.BlockSpec((tm, tn), lambda i, j, kk: (i, j)),
        out_shape=jax.ShapeDtypeStruct((m, n), out_dtype),
        scratch_shapes=scratch,
        compiler_params=_params(("arbitrary", "arbitrary", "arbitrary")),
        name="matmul",
    )(a, w)


def _attn_kernel(q_ref, k_ref, v_ref, o_ref, acc_ref, carry_ref, *, tq, tk):
    i = pl.program_id(1)
    r2 = lax.broadcasted_iota(jnp.int32, (tk, tk), 0)
    c2 = lax.broadcasted_iota(jnp.int32, (tk, tk), 1)
    later = jnp.where(r2 > c2, 1.0, 0.0).astype(BF16)

    acc_ref[...] = jnp.zeros(acc_ref.shape, F32)
    carry_ref[...] = jnp.zeros(carry_ref.shape, F32)

    def span(k_start, nsub, r0, r1, diag):
        kk = k_ref[pl.ds(k_start, nsub * tk), :]
        vv = v_ref[pl.ds(k_start, nsub * tk), :]
        n = lax.dot_general(q_ref[r0:r1, :], kk, (((1,), (1,)), ((), ())),
                            preferred_element_type=F32)
        neg_abs = lax.bitcast_convert_type(
            lax.bitcast_convert_type(n, jnp.uint32) | jnp.uint32(0x80000000), F32)
        lk = jnp.minimum(n, 0.0) - jnp.log(1.0 + jnp.exp2(neg_abs)) * LOG2E
        if diag:
            row = lax.broadcasted_iota(jnp.int32, n.shape, 0)
            col = lax.broadcasted_iota(jnp.int32, n.shape, 1)
            vis = col < row
            lk = jnp.where(vis, lk, 0.0)
        carry = carry_ref[r0:r1, :]
        ws = [None] * nsub
        for s in range(nsub - 1, -1, -1):
            lk_s = lk[:, s * tk:(s + 1) * tk]
            la = jnp.dot(lk_s.astype(BF16), later, preferred_element_type=F32) + carry
            ws[s] = jnp.exp2((la + lk_s) - n[:, s * tk:(s + 1) * tk])
            carry = la[:, 0:1] + lk_s[:, 0:1]
        w = ws[0] if nsub == 1 else jnp.concatenate(ws, axis=1)
        if diag:
            w = jnp.where(vis, w, 0.0)
        acc_ref[r0:r1, :] += jnp.dot(w.astype(BF16), vv, preferred_element_type=F32)
        carry_ref[r0:r1, :] = carry

    q0 = i * tq
    for d in range(tq // tk - 1, -1, -1):
        span(pl.multiple_of(q0 + d * tk, tk), 1, d * tk, tq, True)

    nsub = math.gcd(tq // tk, 2)
    nstep = q0 // (nsub * tk)

    def more(state):
        t, top = state
        return jnp.logical_and(t < nstep, top > F32_UNDERFLOW_LOG2)

    def body(state):
        t, _ = state
        span(pl.multiple_of((nstep - 1 - t) * (nsub * tk), nsub * tk), nsub, 0, tq, False)
        return t + 1, jnp.max(carry_ref[...])

    lax.while_loop(more, body, (jnp.int32(0), jnp.max(carry_ref[...])))
    o_ref[...] = acc_ref[...].astype(o_ref.dtype)


def _attention(q, kv, tq=1024, tk=256):
    l, d = q.shape
    nh = d // SB_HEAD_DIM
    tq = _tile(l, tq)
    tk = _tile(tq, tk)
    return pl.pallas_call(
        functools.partial(_attn_kernel, tq=tq, tk=tk),
        grid=(nh, l // tq),
        in_specs=[
            pl.BlockSpec((tq, SB_HEAD_DIM), lambda h, i: (i, h)),
            pl.BlockSpec((l, SB_HEAD_DIM), lambda h, i: (0, h)),
            pl.BlockSpec((l, SB_HEAD_DIM), lambda h, i: (0, nh + h)),
        ],
        out_specs=pl.BlockSpec((tq, SB_HEAD_DIM), lambda h, i: (i, h)),
        out_shape=jax.ShapeDtypeStruct((l, d), BF16),
        scratch_shapes=[pltpu.VMEM((tq, SB_HEAD_DIM), F32), pltpu.VMEM((tq, 1), F32)],
        compiler_params=_params(("arbitrary", "arbitrary")),
        name="sb_attention",
    )(q, kv, kv)


def _conv_kernel(u_ref, w_ref, b_ref, o_ref, ext_ref, *, tl):
    halo = SUBLANES

    @pl.when(pl.program_id(1) == 0)
    def _():
        ext_ref[0:halo, :] = jnp.zeros((halo, ext_ref.shape[1]), F32)

    u = u_ref[...]
    ext_ref[halo:halo + tl, :] = u
    acc = b_ref[...] + w_ref[SSD_CONV - 1:SSD_CONV, :] * u
    for k in range(SSD_CONV - 1):
        off = halo - (SSD_CONV - 1) + k
        acc = acc + w_ref[k:k + 1, :] * ext_ref[off:off + tl, :]
    o_ref[...] = acc * _sigmoid(acc)
    ext_ref[0:halo, :] = u[tl - halo:tl, :]


def _conv_silu(u, w, b, tl=512, tc=1024):
    l, c = u.shape
    tl, tc = _tile(l, tl), _tile(c, tc)
    return pl.pallas_call(
        functools.partial(_conv_kernel, tl=tl),
        grid=(c // tc, l // tl),
        in_specs=[
            pl.BlockSpec((tl, tc), lambda j, i: (i, j)),
            pl.BlockSpec((SSD_CONV, tc), lambda j, i: (0, j)),
            pl.BlockSpec((1, tc), lambda j, i: (0, j)),
        ],
        out_specs=pl.BlockSpec((tl, tc), lambda j, i: (i, j)),
        out_shape=jax.ShapeDtypeStruct((l, c), F32),
        scratch_shapes=[pltpu.VMEM((tl + SUBLANES, tc), F32)],
        compiler_params=_params(("arbitrary", "arbitrary")),
        name="ssd_conv",
    )(u, w, b.reshape(1, c))


def _split3(x):
    p1 = x.astype(BF16)
    r1 = x - p1.astype(F32)
    p2 = r1.astype(BF16)
    p3 = (r1 - p2.astype(F32)).astype(BF16)
    return p1, p2, p3


def _dtprep_kernel(dt_ref, bias_ref, alog_ref, dtr_ref, acsr_ref, acsc_ref, *, hpg):
    dt = _softplus(dt_ref[...] + bias_ref[...])
    adt = dt * (-jnp.exp(alog_ref[...]))
    n = dt.shape[0]
    r = lax.broadcasted_iota(jnp.int32, (n, n), 0)
    c = lax.broadcasted_iota(jnp.int32, (n, n), 1)
    upto = jnp.where(c <= r, 1.0, 0.0).astype(BF16)
    acs = sum(jnp.dot(upto, p, preferred_element_type=F32) for p in _split3(adt))
    dtr_ref[...] = dt.T
    acsr_ref[...] = acs.T
    for g in range(acsc_ref.shape[0]):
        acsc_ref[g] = acs[:, g * hpg:(g + 1) * hpg]


def _dtprep(dt_raw, dt_bias, a_log):
    l, nh = dt_raw.shape
    hpg = nh // SSD_GROUPS
    ch = SSD_CHUNK
    return pl.pallas_call(
        functools.partial(_dtprep_kernel, hpg=hpg),
        grid=(l // ch,),
        in_specs=[
            pl.BlockSpec((ch, nh), lambda i: (i, 0)),
            pl.BlockSpec((1, nh), lambda i: (0, 0)),
            pl.BlockSpec((1, nh), lambda i: (0, 0)),
        ],
        out_specs=[
            pl.BlockSpec((nh, ch), lambda i: (0, i)),
            pl.BlockSpec((nh, ch), lambda i: (0, i)),
            pl.BlockSpec((SSD_GROUPS, ch, hpg), lambda i: (0, i, 0)),
        ],
        out_shape=[
            jax.ShapeDtypeStruct((nh, l), F32),
            jax.ShapeDtypeStruct((nh, l), F32),
            jax.ShapeDtypeStruct((SSD_GROUPS, l, hpg), F32),
        ],
        compiler_params=_params(("arbitrary",)),
        name="ssd_dtprep",
    )(dt_raw, dt_bias.reshape(1, nh), a_log.reshape(1, nh))


def _ssd_kernel(xs_ref, b_ref, c_ref, dtr_ref, acsr_ref, acsc_ref, z_ref, dexp_ref, nw_ref,
                o_ref, st_ref, y_ref, *, npairs):
    ch = SSD_CHUNK
    hp = SSD_HEAD_DIM

    @pl.when(pl.program_id(1) == 0)
    def _():
        st_ref[...] = jnp.zeros(st_ref.shape, F32)

    bm = b_ref[...]
    cm = c_ref[...]
    cb = lax.dot_general(cm.astype(BF16), bm.astype(BF16), (((1,), (1,)), ((), ())),
                         preferred_element_type=F32)
    bt = bm.T
    row = lax.broadcasted_iota(jnp.int32, (ch, ch), 0)
    col = lax.broadcasted_iota(jnp.int32, (ch, ch), 1)
    causal = col <= row
    first = lax.broadcasted_iota(jnp.int32, (ch, 2 * hp), 1) < hp
    dtr = dtr_ref[...]
    acsr = acsr_ref[...]
    acsc = acsc_ref[0]

    for pr in range(npairs):
        sl = slice(pr * 2 * hp, (pr + 1) * 2 * hp)
        xs_b = xs_ref[:, sl].astype(BF16)
        prev = st_ref[pr]
        rhs = jnp.concatenate([xs_b, prev.astype(BF16)], axis=0)
        outs, news = [], []
        for hh in (2 * pr, 2 * pr + 1):
            a_col = jnp.broadcast_to(acsc[:, hh:hh + 1], (ch, ch))
            a_row = acsr[hh:hh + 1, :]
            dt_row = dtr[hh:hh + 1, :]
            lmat = jnp.exp(jnp.where(causal, a_col - a_row, NEG_INF))
            scores = cb * lmat * dt_row
            c_dec = cm * jnp.exp(a_col)
            lhs = jnp.concatenate([scores.astype(BF16), c_dec.astype(BF16)], axis=1)
            outs.append(jnp.dot(lhs, rhs, preferred_element_type=F32))
            a_last = a_row[:, ch - 1:ch]
            b_dec = (bt * (dt_row * jnp.exp(a_last - a_row))).astype(BF16)
            news.append(jnp.exp(a_last) * prev
                        + jnp.dot(b_dec, xs_b, preferred_element_type=F32))
        y_ref[:, sl] = jnp.where(first, outs[0], outs[1])
        st_ref[pr] = jnp.where(first, news[0], news[1])

    y = y_ref[...] + dexp_ref[...] * xs_ref[...]
    z = z_ref[...]
    yg = y * (z * _sigmoid(z))
    o_ref[...] = (_rms(yg) * nw_ref[...]).astype(o_ref.dtype)


def _ssd_scan(xbc, z, dtr, acsr, acsc, d_exp, norm_w):
    l, d_inner = z.shape
    gw = d_inner // SSD_GROUPS
    npairs = gw // (2 * SSD_HEAD_DIM)
    hpg = gw // SSD_HEAD_DIM
    ch = SSD_CHUNK
    nb = d_inner // SSD_STATE
    return pl.pallas_call(
        functools.partial(_ssd_kernel, npairs=npairs),
        grid=(SSD_GROUPS, l // ch),
        in_specs=[
            pl.BlockSpec((ch, gw), lambda g, c: (c, g)),
            pl.BlockSpec((ch, SSD_STATE), lambda g, c: (c, nb + g)),
            pl.BlockSpec((ch, SSD_STATE), lambda g, c: (c, nb + SSD_GROUPS + g)),
            pl.BlockSpec((hpg, ch), lambda g, c: (g, c)),
            pl.BlockSpec((hpg, ch), lambda g, c: (g, c)),
            pl.BlockSpec((1, ch, hpg), lambda g, c: (g, c, 0)),
            pl.BlockSpec((ch, gw), lambda g, c: (c, g)),
            pl.BlockSpec((1, gw), lambda g, c: (0, g)),
            pl.BlockSpec((1, gw), lambda g, c: (0, g)),
        ],
        out_specs=pl.BlockSpec((ch, gw), lambda g, c: (c, g)),
        out_shape=jax.ShapeDtypeStruct((l, d_inner), BF16),
        scratch_shapes=[pltpu.VMEM((npairs, SSD_STATE, 2 * SSD_HEAD_DIM), F32),
                        pltpu.VMEM((ch, gw), F32)],
        compiler_params=_params(("arbitrary", "arbitrary")),
        name="ssd_scan",
    )(xbc, xbc, xbc, dtr, acsr, acsc, z, d_exp, norm_w.reshape(1, d_inner))


def _deinterleave_kernel(w_ref, p_ref, o_ref):
    o_ref[0] = jnp.dot(w_ref[0, 0].astype(BF16), p_ref[...],
                       preferred_element_type=F32).astype(BF16)


def _deinterleave_cast(w_gu, layer, tr=1024):
    _, ne, d, f2 = w_gu.shape
    tr = _tile(d, tr)
    src = jnp.arange(f2)
    dst = jnp.where(src % 2 == 0, src // 2, f2 // 2 + src // 2)
    perm = (dst[:, None] == jnp.arange(f2)[None, :]).astype(BF16)
    return pl.pallas_call(
        _deinterleave_kernel,
        grid=(ne, d // tr),
        in_specs=[pl.BlockSpec((1, 1, tr, f2), lambda e, i: (layer, e, i, 0)),
                  pl.BlockSpec((f2, f2), lambda e, i: (0, 0))],
        out_specs=pl.BlockSpec((1, tr, f2), lambda e, i: (e, i, 0)),
        out_shape=jax.ShapeDtypeStruct((ne, d, f2), BF16),
        compiler_params=_params(("arbitrary", "arbitrary")),
        name="moe_deinterleave",
    )(w_gu, perm)


def _lookup(table, idx):
    hit = idx[:, None] == jnp.arange(table.shape[0], dtype=jnp.int32)[None, :]
    return jnp.sum(jnp.where(hit, table[None, :], 0), axis=1)


def _route_tables(route, ne, tr, tm):
    t = route.shape[0]
    a = t * TOP_K
    eidx = route[:, :TOP_K].astype(jnp.int32).reshape(a)
    order = jnp.argsort(eidx, stable=True).astype(jnp.int32)
    rank = jnp.argsort(order).astype(jnp.int32)
    counts = jnp.sum(eidx[:, None] == jnp.arange(ne, dtype=jnp.int32)[None, :], axis=0,
                     dtype=jnp.int32)
    starts = jnp.cumsum(counts) - counts
    padded = (counts + tr - 1) // tr * tr
    pends = jnp.cumsum(padded)
    pstarts = pends - padded
    pos = rank + _lookup(pstarts - starts, eidx)
    n_tiles = (a + ne * tr) // tr
    used = (pends[-1] // tr).astype(jnp.int32)
    tile0 = jnp.arange(n_tiles, dtype=jnp.int32) * tr
    tile0_used = jnp.minimum(tile0, (used - 1) * tr)
    tile_e = jnp.sum(pends[None, :] <= tile0_used[:, None], axis=1, dtype=jnp.int32)
    src_base = jnp.clip(_lookup(starts - pstarts, tile_e) + tile0_used, 0, a)
    order_tok = jnp.concatenate([order // TOP_K, jnp.zeros((tr,), jnp.int32)])
    pos_blk = pos.reshape(t // tm, tm, TOP_K).transpose(0, 2, 1).reshape(a)
    return order_tok, tile_e, src_base, used.reshape(1), pos_blk


def _moe_kernel(tile_e_ref, base_ref, used_ref, tok_ref, h_hbm, wgl_ref, wd_ref, bg_ref, bl_ref,
                bd_ref, o_ref, buf, sem, *, tr):
    t = pl.program_id(0)
    used = used_ref[0]
    slot = t % 2
    f = bg_ref.shape[-1]

    @pl.when(t == 0)
    def _():
        _gather_start(h_hbm, tok_ref, base_ref[0], tr, buf.at[0], sem.at[0])

    @pl.when(t < used)
    def _():
        _gather_wait(h_hbm, tr, buf.at[slot], sem.at[slot])
        nxt_base = base_ref[jnp.minimum(t + 1, used - 1)]
        for r in range(tr):
            _row_copy(h_hbm, tok_ref[nxt_base + r], buf.at[1 - slot], r, sem.at[1 - slot]).start()
        gl = jnp.dot(_unpack_bf16_pairs(buf[slot]).astype(BF16), wgl_ref[0],
                     preferred_element_type=F32)
        xg = jnp.minimum(gl[:, :f] + bg_ref[0], SWIGLU_LIMIT)
        xl = jnp.clip(gl[:, f:] + bl_ref[0], -SWIGLU_LIMIT, SWIGLU_LIMIT)
        act = xg * _sigmoid(SWIGLU_ALPHA * xg) * (xl + 1.0)
        o_ref[...] = _pack_bf16_pairs(jnp.dot(act.astype(BF16), wd_ref[0, 0].astype(BF16),
                                              preferred_element_type=F32) + bd_ref[0])

    @pl.when(t + 1 == used)
    def _():
        _gather_wait(h_hbm, tr, buf.at[1 - slot], sem.at[1 - slot])

    @pl.when(t >= used)
    def _():
        o_ref[...] = jnp.zeros(o_ref.shape, o_ref.dtype)


def _moe_sparse(h, w_gl, w_down, layer, b_g, b_l, b_down, order_tok, tile_e, src_base, used, tr):
    ne, d, f2 = w_gl.shape
    f = f2 // 2
    n_tiles = tile_e.shape[0]
    return pl.pallas_call(
        functools.partial(_moe_kernel, tr=tr),
        grid_spec=pltpu.PrefetchScalarGridSpec(
            num_scalar_prefetch=4,
            grid=(n_tiles,),
            in_specs=[
                pl.BlockSpec(memory_space=pl.ANY),
                pl.BlockSpec((1, d, f2), lambda i, te, *_: (te[i], 0, 0)),
                pl.BlockSpec((1, 1, f, d), lambda i, te, *_: (layer, te[i], 0, 0)),
                pl.BlockSpec((1, 1, f), lambda i, te, *_: (te[i], 0, 0)),
                pl.BlockSpec((1, 1, f), lambda i, te, *_: (te[i], 0, 0)),
                pl.BlockSpec((1, 1, d), lambda i, te, *_: (te[i], 0, 0)),
            ],
            out_specs=pl.BlockSpec((tr, d // 2), lambda i, *_: (i, 0)),
            scratch_shapes=[pltpu.VMEM((2, tr, d // 2), jnp.uint32),
                            pltpu.SemaphoreType.DMA((2,))],
        ),
        out_shape=jax.ShapeDtypeStruct((n_tiles * tr, d // 2), jnp.uint32),
        compiler_params=_params(("arbitrary",)),
        name="moe_experts",
    )(tile_e, src_base, used, order_tok, h, w_gl, w_down, b_g, b_l, b_down)


def kernel(x, c, w_cond, b_cond, w_mod, b_mod, g_mix_pre, g_mix_post, g_ffn_pre, g_ffn_post, attn_w_in, attn_w_out, ssd_w_in, ssd_conv_w, ssd_conv_b, ssd_dt_bias, ssd_a_log, ssd_d, ssd_norm_w, ssd_w_out, moe_w_router, moe_b_router, moe_w_gu, moe_b_gu, moe_w_down, moe_b_down):
    bsz, seq, d = x.shape
    assert bsz == 1, "kernel is written for a single sequence"
    depth = w_mod.shape[0]
    ne = moe_w_router.shape[-1]
    ff = moe_w_down.shape[2]
    d_inner = ssd_w_out.shape[1]
    conv_dim = ssd_conv_w.shape[-1]

    mod = _conditioning(c, w_cond, b_cond, w_mod, b_mod)

    def mods(i):
        return [mod[i, k * d:(k + 1) * d] for k in range(N_MOD)]

    xt = x.reshape(seq, d)
    tm = _tile(seq, GLUE_ROWS)
    sh_m, sc_m, ga_m, sh_f, sc_f, ga_f = mods(0)
    (h,) = _glue(xt, nxt=(g_mix_pre[0], sc_m, sh_m), tm=tm)

    for i in range(depth):
        j = i // 2
        if i % 2 == 0:
            q = _matmul(h, attn_w_in, j, BF16, col0=0, ncols=d,
                        scale=-LOG2E * SB_HEAD_DIM ** -0.5)
            kv = _matmul(h, attn_w_in, j, BF16, col0=d, ncols=2 * d)
            o = _attention(q, kv)
            y = _matmul(o, attn_w_out, j, F32)
        else:
            z = _matmul(h, ssd_w_in, j, F32, col0=0, ncols=d_inner)
            xbc = _matmul(h, ssd_w_in, j, F32, col0=d_inner, ncols=conv_dim)
            dt_raw = _matmul(h, ssd_w_in, j, F32, col0=d_inner + conv_dim)
            xbc = _conv_silu(xbc, ssd_conv_w[j], ssd_conv_b[j])
            dtr, acsr, acsc = _dtprep(dt_raw, ssd_dt_bias[j], ssd_a_log[j])
            d_exp = jnp.repeat(ssd_d[j], SSD_HEAD_DIM).reshape(1, d_inner)
            yn = _ssd_scan(xbc, z, dtr, acsr, acsc, d_exp, ssd_norm_w[j])
            y = _matmul(yn, ssd_w_out, j, F32)

        wr = jnp.pad(moe_w_router[i], ((0, 0), (0, LANES - ne)))
        wr_hi = wr.astype(BF16)
        wr = jnp.stack([wr_hi, (wr - wr_hi.astype(F32)).astype(BF16)])
        br = jnp.pad(moe_b_router[i], (0, LANES - ne), constant_values=NEG_INF).reshape(1, LANES)
        xt, h, route = _glue(xt, resid=(y, ga_m, g_mix_post[i]),
                             nxt=(g_ffn_pre[i], sc_f, sh_f), router=(wr, br), tm=tm)

        order_tok, tile_e, src_base, used, pos_blk = _route_tables(route, ne, MOE_TILE_ROWS, tm)
        w_gl = _deinterleave_cast(moe_w_gu, i)
        b_g = moe_b_gu[i][:, 0::2].reshape(ne, 1, ff)
        b_l = moe_b_gu[i][:, 1::2].reshape(ne, 1, ff)
        y_rows = _moe_sparse(h, w_gl, moe_w_down, i, b_g, b_l, moe_b_down[i].reshape(ne, 1, d),
                             order_tok, tile_e, src_base, used, MOE_TILE_ROWS)

        ga_prev = ga_f
        if i + 1 < depth:
            sh_m, sc_m, ga_m, sh_f, sc_f, ga_f = mods(i + 1)
            xt, h = _glue(xt, resid=(y_rows, ga_prev, g_ffn_post[i]),
                          nxt=(g_mix_pre[i + 1], sc_m, sh_m), gather=(pos_blk, route), tm=tm)
        else:
            (xt,) = _glue(xt, resid=(y_rows, ga_prev, g_ffn_post[i]), gather=(pos_blk, route),
                          tm=tm)

    return xt.reshape(bsz, seq, d)
```

```python
import functools
import math

import jax
import jax.numpy as jnp
from jax import lax
from jax.experimental import pallas as pl
from jax.experimental.pallas import tpu as pltpu

SB_HEAD_DIM = 128
SSD_HEAD_DIM = 64
SSD_GROUPS = 8
SSD_STATE = 128
SSD_CONV = 4
SSD_CHUNK = 128
TOP_K = 4
SWIGLU_ALPHA = 1.702
SWIGLU_LIMIT = 7.0
N_MOD = 6
NORM_EPS = 1e-6

LANES = 128
SUBLANES = 8
VMEM_LIMIT_BYTES = 56 * 1024 * 1024

GLUE_ROWS = 128
MOE_TILE_ROWS = 256
GATHER_UNROLL = 8

F32 = jnp.float32
BF16 = jnp.bfloat16
HIGHEST = lax.Precision.HIGHEST
NEG_INF = float("-inf")
LOG2E = math.log2(math.e)
F32_UNDERFLOW_LOG2 = -160.0


def _params(sem):
    return pltpu.CompilerParams(dimension_semantics=sem, vmem_limit_bytes=VMEM_LIMIT_BYTES)


def _sigmoid(x):
    return 1.0 / (1.0 + jnp.exp(-x))


def _softplus(x):
    return jnp.maximum(x, 0.0) + jnp.log(1.0 + jnp.exp(-jnp.abs(x)))


def _tile(n, pref):
    if n <= pref:
        return n
    t = pref
    while n % t:
        t //= 2
    return t


def _cond_kernel(c_ref, w_ref, b_ref, o_ref):
    c = c_ref[...]
    s = c * _sigmoid(c)
    r = jnp.dot(s, w_ref[...], precision=HIGHEST, preferred_element_type=F32) + b_ref[...]
    o_ref[...] = r * _sigmoid(r)


def _mod_kernel(cond_ref, w_ref, b_ref, o_ref):
    o_ref[0] = jnp.dot(cond_ref[...], w_ref[0], precision=HIGHEST,
                       preferred_element_type=F32) + b_ref[0]


def _conditioning(c, w_cond, b_cond, w_mod, b_mod):
    bsz, d = c.shape
    rank = w_cond.shape[1]
    depth, _, nmod = w_mod.shape
    rows = SUBLANES
    c8 = jnp.broadcast_to(c[:1], (rows, d))
    cond = pl.pallas_call(
        _cond_kernel,
        out_shape=jax.ShapeDtypeStruct((rows, rank), F32),
        compiler_params=_params(None),
        name="cond_trunk",
    )(c8, w_cond, b_cond.reshape(1, rank))
    tn = _tile(nmod, 4096)
    mod = pl.pallas_call(
        _mod_kernel,
        grid=(depth, nmod // tn),
        in_specs=[
            pl.BlockSpec((rows, rank), lambda i, j: (0, 0)),
            pl.BlockSpec((1, rank, tn), lambda i, j: (i, 0, j)),
            pl.BlockSpec((1, 1, tn), lambda i, j: (i, 0, j)),
        ],
        out_specs=pl.BlockSpec((1, rows, tn), lambda i, j: (i, 0, j)),
        out_shape=jax.ShapeDtypeStruct((depth, rows, nmod), F32),
        compiler_params=_params(("arbitrary", "arbitrary")),
        name="cond_mod",
    )(cond, w_mod, b_mod.reshape(depth, 1, nmod))
    return mod[:, 0, :]


def _rms(y):
    return y * lax.rsqrt(jnp.mean(y * y, axis=-1, keepdims=True) + NORM_EPS)


def _pack_bf16_pairs(v):
    half = v.shape[1] // 2
    bits = lax.bitcast_convert_type(v.astype(BF16).astype(F32), jnp.uint32)
    return (bits[:, :half] >> 16) | bits[:, half:]


def _unpack_bf16_pairs(w):
    lo = lax.bitcast_convert_type(w << 16, F32)
    hi = lax.bitcast_convert_type(w & jnp.uint32(0xFFFF0000), F32)
    return jnp.concatenate([lo, hi], axis=1)


def _row_copy(src_hbm, row, dst, r, sem):
    return pltpu.make_async_copy(src_hbm.at[pl.ds(row, 1)], dst.at[pl.ds(r, 1)], sem)


def _gather_start(src_hbm, idx_ref, base, n, dst, sem):
    def body(r, carry):
        _row_copy(src_hbm, idx_ref[base + r], dst, r, sem).start()
        return carry

    lax.fori_loop(0, n, body, 0, unroll=GATHER_UNROLL)


def _gather_wait(src_hbm, n, dst, sem):
    def body(r, carry):
        _row_copy(src_hbm, 0, dst, r, sem).wait()
        return carry

    lax.fori_loop(0, n, body, 0, unroll=GATHER_UNROLL)


def _glue_kernel(*refs, has_resid, has_next, has_router, gathered):
    it = iter(refs)
    if gathered:
        pos_ref = next(it)
    x_ref = next(it)
    if gathered:
        rw_ref = next(it)
    if has_resid:
        y_ref, ga_ref, gpost_ref = next(it), next(it), next(it)
    if has_next:
        gpre_ref, sc_ref, sh_ref = next(it), next(it), next(it)
    if has_router:
        wr_ref, br_ref = next(it), next(it)
    if has_resid:
        xo_ref = next(it)
    if has_next:
        h_ref = next(it)
    if has_router:
        route_ref = next(it)
    if gathered:
        buf, sem = next(it), next(it)

    x = x_ref[...]
    if has_resid:
        if gathered:
            i = pl.program_id(0)
            tm = x.shape[0]
            nrow = TOP_K * tm
            slot = i % 2

            @pl.when(i == 0)
            def _():
                _gather_start(y_ref, pos_ref, 0, nrow, buf.at[0], sem.at[0])

            _gather_wait(y_ref, nrow, buf.at[slot], sem.at[slot])
            last = pl.num_programs(0) - 1
            nxt_base = jnp.minimum(i + 1, last) * nrow
            for r in range(nrow):
                _row_copy(y_ref, pos_ref[nxt_base + r], buf.at[1 - slot], r,
                          sem.at[1 - slot]).start()
            rw = rw_ref[...]
            y = rw[:, TOP_K:TOP_K + 1] * _unpack_bf16_pairs(buf[slot, 0:tm, :])
            for k in range(1, TOP_K):
                y = y + (rw[:, TOP_K + k:TOP_K + k + 1]
                         * _unpack_bf16_pairs(buf[slot, k * tm:(k + 1) * tm, :]))
        else:
            y = y_ref[...]
        x = x + ga_ref[...] * (_rms(y) * gpost_ref[...])
        xo_ref[...] = x
    if has_next:
        h = (_rms(x) * gpre_ref[...]) * (1.0 + sc_ref[...]) + sh_ref[...]
        if has_router:
            h_ref[...] = _pack_bf16_pairs(h)
        else:
            h_ref[...] = h.astype(h_ref.dtype)
    if has_router:
        h_hi = h.astype(BF16)
        h_lo = (h - h_hi.astype(F32)).astype(BF16)
        w_hi, w_lo = wr_ref[0], wr_ref[1]
        logits = (jnp.dot(h_hi, w_hi, preferred_element_type=F32)
                  + (jnp.dot(h_lo, w_hi, preferred_element_type=F32)
                     + jnp.dot(h_hi, w_lo, preferred_element_type=F32))) + br_ref[...]
        lane = lax.broadcasted_iota(jnp.int32, logits.shape, 1).astype(F32)
        work = logits
        vals, idxs = [], []
        for _ in range(TOP_K):
            m = jnp.max(work, axis=1, keepdims=True)
            idx = jnp.min(jnp.where(work == m, lane, float(LANES)), axis=1, keepdims=True)
            vals.append(m)
            idxs.append(idx)
            work = jnp.where(lane == idx, NEG_INF, work)
        exps = [jnp.exp(v - vals[0]) for v in vals]
        denom = exps[0]
        for e in exps[1:]:
            denom = denom + e
        route = jnp.zeros_like(logits)
        for j in range(TOP_K):
            route = jnp.where(lane == float(j), idxs[j], route)
            route = jnp.where(lane == float(TOP_K + j), exps[j] / denom, route)
        route_ref[...] = route
    if gathered:
        @pl.when(i == last)
        def _():
            _gather_wait(y_ref, nrow, buf.at[1 - slot], sem.at[1 - slot])


def _glue(x, resid=None, nxt=None, router=None, gather=None, tm=GLUE_ROWS):
    t, d = x.shape
    tm = _tile(t, tm)
    gathered = gather is not None
    row = pl.BlockSpec((tm, d), lambda i, *_: (i, 0))
    vec = pl.BlockSpec((1, d), lambda i, *_: (0, 0))
    lanes = pl.BlockSpec((tm, LANES), lambda i, *_: (i, 0))
    args, in_specs, out_shape, out_specs, scratch = [x], [row], [], [], []
    if gathered:
        args.append(gather[1])
        in_specs.append(lanes)
    if resid is not None:
        y, ga, gpost = resid
        args += [y, ga.reshape(1, d), gpost.reshape(1, d)]
        in_specs += [pl.BlockSpec(memory_space=pl.ANY) if gathered else row, vec, vec]
        out_shape.append(jax.ShapeDtypeStruct((t, d), F32))
        out_specs.append(row)
    if nxt is not None:
        gpre, sc, sh = nxt
        args += [gpre.reshape(1, d), sc.reshape(1, d), sh.reshape(1, d)]
        in_specs += [vec, vec, vec]
        if router is None:
            out_shape.append(jax.ShapeDtypeStruct((t, d), BF16))
            out_specs.append(row)
        else:
            out_shape.append(jax.ShapeDtypeStruct((t, d // 2), jnp.uint32))
            out_specs.append(pl.BlockSpec((tm, d // 2), lambda i, *_: (i, 0)))
    if router is not None:
        wr, br = router
        args += [wr, br]
        in_specs += [pl.BlockSpec((2, d, LANES), lambda i, *_: (0, 0, 0)),
                     pl.BlockSpec((1, LANES), lambda i, *_: (0, 0))]
        out_shape.append(jax.ShapeDtypeStruct((t, LANES), F32))
        out_specs.append(lanes)
    if gathered:
        args = [gather[0]] + args
        scratch = [pltpu.VMEM((2, TOP_K * tm, d // 2), jnp.uint32),
                   pltpu.SemaphoreType.DMA((2,))]
    outs = pl.pallas_call(
        functools.partial(_glue_kernel, has_resid=resid is not None, has_next=nxt is not None,
                          has_router=router is not None, gathered=gathered),
        grid_spec=pltpu.PrefetchScalarGridSpec(
            num_scalar_prefetch=1 if gathered else 0,
            grid=(t // tm,),
            in_specs=in_specs,
            out_specs=out_specs,
            scratch_shapes=scratch,
        ),
        out_shape=out_shape,
        compiler_params=_params(("arbitrary",)),
        name="glue_gather" if gathered else "glue",
    )(*args)
    return outs


def _mm_kernel(a_ref, w_ref, o_ref, *scratch, nk, scale):
    part = jnp.dot(a_ref[...], w_ref[0].astype(BF16), preferred_element_type=F32)
    if nk == 1:
        if scale != 1.0:
            part = part * scale
        o_ref[...] = part.astype(o_ref.dtype)
        return
    acc_ref = scratch[0] if scratch else o_ref
    k = pl.program_id(2)

    @pl.when(k == 0)
    def _():
        acc_ref[...] = part

    @pl.when(k > 0)
    def _():
        acc_ref[...] += part

    if scratch or scale != 1.0:
        @pl.when(k == nk - 1)
        def _():
            r = acc_ref[...]
            if scale != 1.0:
                r = r * scale
            o_ref[...] = r.astype(o_ref.dtype)


def _matmul(a, w, layer, out_dtype, col0=0, ncols=None, scale=1.0, tm=1024, tn=512, tk=4096):
    m, k = a.shape
    n = w.shape[2] - col0 if ncols is None else ncols
    tm, tn, tk = _tile(m, tm), _tile(n, tn), _tile(k, tk)
    assert col0 % tn == 0 and n % tn == 0
    nk = k // tk
    jb = col0 // tn
    scratch = []
    if nk > 1 and out_dtype != F32:
        scratch = [pltpu.VMEM((tm, tn), F32)]
    return pl.pallas_call(
        functools.partial(_mm_kernel, nk=nk, scale=scale),
        grid=(m // tm, n // tn, nk),
        in_specs=[pl.BlockSpec((tm, tk), lambda i, j, kk: (i, kk)),
                  pl.BlockSpec((1, tk, tn), lambda i, j, kk: (layer, kk, jb + j))],
        out_specs=pl.BlockSpec((tm, tn), lambda i, j, kk: (i, j)),
        out_shape=jax.ShapeDtypeStruct((m, n), out_dtype),
        scratch_shapes=scratch,
        compiler_params=_params(("arbitrary", "arbitrary", "arbitrary")),
        name="matmul",
    )(a, w)


def _attn_kernel(q_ref, k_ref, v_ref, o_ref, acc_ref, carry_ref, *, tq, tk):
    i = pl.program_id(1)
    r2 = lax.broadcasted_iota(jnp.int32, (tk, tk), 0)
    c2 = lax.broadcasted_iota(jnp.int32, (tk, tk), 1)
    later = jnp.where(r2 > c2, 1.0, 0.0).astype(BF16)

    acc_ref[...] = jnp.zeros(acc_ref.shape, F32)
    carry_ref[...] = jnp.zeros(carry_ref.shape, F32)

    def span(k_start, nsub, r0, r1, diag):
        kk = k_ref[pl.ds(k_start, nsub * tk), :]
        vv = v_ref[pl.ds(k_start, nsub * tk), :]
        n = lax.dot_general(q_ref[r0:r1, :], kk, (((1,), (1,)), ((), ())),
                            preferred_element_type=F32)
        neg_abs = lax.bitcast_convert_type(
            lax.bitcast_convert_type(n, jnp.uint32) | jnp.uint32(0x80000000), F32)
        lk = jnp.minimum(n, 0.0) - jnp.log(1.0 + jnp.exp2(neg_abs)) * LOG2E
        if diag:
            row = lax.broadcasted_iota(jnp.int32, n.shape, 0)
            col = lax.broadcasted_iota(jnp.int32, n.shape, 1)
            vis = col < row
            lk = jnp.where(vis, lk, 0.0)
        carry = carry_ref[r0:r1, :]
        ws = [None] * nsub
        for s in range(nsub - 1, -1, -1):
            lk_s = lk[:, s * tk:(s + 1) * tk]
            la = jnp.dot(lk_s.astype(BF16), later, preferred_element_type=F32) + carry
            ws[s] = jnp.exp2((la + lk_s) - n[:, s * tk:(s + 1) * tk])
            carry = la[:, 0:1] + lk_s[:, 0:1]
        w = ws[0] if nsub == 1 else jnp.concatenate(ws, axis=1)
        if diag:
            w = jnp.where(vis, w, 0.0)
        acc_ref[r0:r1, :] += jnp.dot(w.astype(BF16), vv, preferred_element_type=F32)
        carry_ref[r0:r1, :] = carry

    q0 = i * tq
    for d in range(tq // tk - 1, -1, -1):
        span(pl.multiple_of(q0 + d * tk, tk), 1, d * tk, tq, True)

    nsub = 1
    nstep = q0 // (nsub * tk)

    def more(state):
        t, top = state
        return jnp.logical_and(t < nstep, top > F32_UNDERFLOW_LOG2)

    def body(state):
        t, _ = state
        span(pl.multiple_of((nstep - 1 - t) * (nsub * tk), nsub * tk), nsub, 0, tq, False)
        return t + 1, jnp.max(carry_ref[...])

    lax.while_loop(more, body, (jnp.int32(0), jnp.max(carry_ref[...])))
    o_ref[...] = acc_ref[...].astype(o_ref.dtype)


def _attention(q, kv, tq=1024, tk=256):
    l, d = q.shape
    nh = d // SB_HEAD_DIM
    tq = _tile(l, tq)
    tk = _tile(tq, tk)
    return pl.pallas_call(
        functools.partial(_attn_kernel, tq=tq, tk=tk),
        grid=(nh, l // tq),
        in_specs=[
            pl.BlockSpec((tq, SB_HEAD_DIM), lambda h, i: (i, h)),
            pl.BlockSpec((l, SB_HEAD_DIM), lambda h, i: (0, h)),
            pl.BlockSpec((l, SB_HEAD_DIM), lambda h, i: (0, nh + h)),
        ],
        out_specs=pl.BlockSpec((tq, SB_HEAD_DIM), lambda h, i: (i, h)),
        out_shape=jax.ShapeDtypeStruct((l, d), BF16),
        scratch_shapes=[pltpu.VMEM((tq, SB_HEAD_DIM), F32), pltpu.VMEM((tq, 1), F32)],
        compiler_params=_params(("arbitrary", "arbitrary")),
        name="sb_attention",
    )(q, kv, kv)


def _conv_kernel(u_ref, w_ref, b_ref, o_ref, ext_ref, *, tl):
    halo = SUBLANES

    @pl.when(pl.program_id(1) == 0)
    def _():
        ext_ref[0:halo, :] = jnp.zeros((halo, ext_ref.shape[1]), F32)

    u = u_ref[...]
    ext_ref[halo:halo + tl, :] = u
    acc = b_ref[...] + w_ref[SSD_CONV - 1:SSD_CONV, :] * u
    for k in range(SSD_CONV - 1):
        off = halo - (SSD_CONV - 1) + k
        acc = acc + w_ref[k:k + 1, :] * ext_ref[off:off + tl, :]
    o_ref[...] = acc * _sigmoid(acc)
    ext_ref[0:halo, :] = u[tl - halo:tl, :]


def _conv_silu(u, w, b, tl=512, tc=1024):
    l, c = u.shape
    tl, tc = _tile(l, tl), _tile(c, tc)
    return pl.pallas_call(
        functools.partial(_conv_kernel, tl=tl),
        grid=(c // tc, l // tl),
        in_specs=[
            pl.BlockSpec((tl, tc), lambda j, i: (i, j)),
            pl.BlockSpec((SSD_CONV, tc), lambda j, i: (0, j)),
            pl.BlockSpec((1, tc), lambda j, i: (0, j)),
        ],
        out_specs=pl.BlockSpec((tl, tc), lambda j, i: (i, j)),
        out_shape=jax.ShapeDtypeStruct((l, c), F32),
        scratch_shapes=[pltpu.VMEM((tl + SUBLANES, tc), F32)],
        compiler_params=_params(("arbitrary", "arbitrary")),
        name="ssd_conv",
    )(u, w, b.reshape(1, c))


def _split3(x):
    p1 = x.astype(BF16)
    r1 = x - p1.astype(F32)
    p2 = r1.astype(BF16)
    p3 = (r1 - p2.astype(F32)).astype(BF16)
    return p1, p2, p3


def _dtprep_kernel(dt_ref, bias_ref, alog_ref, dtr_ref, acsr_ref, acsc_ref, *, hpg):
    dt = _softplus(dt_ref[...] + bias_ref[...])
    adt = dt * (-jnp.exp(alog_ref[...]))
    n = dt.shape[0]
    r = lax.broadcasted_iota(jnp.int32, (n, n), 0)
    c = lax.broadcasted_iota(jnp.int32, (n, n), 1)
    upto = jnp.where(c <= r, 1.0, 0.0).astype(BF16)
    acs = sum(jnp.dot(upto, p, preferred_element_type=F32) for p in _split3(adt))
    dtr_ref[...] = dt.T
    acsr_ref[...] = acs.T
    for g in range(acsc_ref.shape[0]):
        acsc_ref[g] = acs[:, g * hpg:(g + 1) * hpg]


def _dtprep(dt_raw, dt_bias, a_log):
    l, nh = dt_raw.shape
    hpg = nh // SSD_GROUPS
    ch = SSD_CHUNK
    return pl.pallas_call(
        functools.partial(_dtprep_kernel, hpg=hpg),
        grid=(l // ch,),
        in_specs=[
            pl.BlockSpec((ch, nh), lambda i: (i, 0)),
            pl.BlockSpec((1, nh), lambda i: (0, 0)),
            pl.BlockSpec((1, nh), lambda i: (0, 0)),
        ],
        out_specs=[
            pl.BlockSpec((nh, ch), lambda i: (0, i)),
            pl.BlockSpec((nh, ch), lambda i: (0, i)),
            pl.BlockSpec((SSD_GROUPS, ch, hpg), lambda i: (0, i, 0)),
        ],
        out_shape=[
            jax.ShapeDtypeStruct((nh, l), F32),
            jax.ShapeDtypeStruct((nh, l), F32),
            jax.ShapeDtypeStruct((SSD_GROUPS, l, hpg), F32),
        ],
        compiler_params=_params(("arbitrary",)),
        name="ssd_dtprep",
    )(dt_raw, dt_bias.reshape(1, nh), a_log.reshape(1, nh))


def _ssd_kernel(xs_ref, b_ref, c_ref, dtr_ref, acsr_ref, acsc_ref, z_ref, dexp_ref, nw_ref,
                o_ref, st_ref, y_ref, *, npairs):
    ch = SSD_CHUNK
    hp = SSD_HEAD_DIM

    @pl.when(pl.program_id(1) == 0)
    def _():
        st_ref[...] = jnp.zeros(st_ref.shape, F32)

    bm = b_ref[...]
    cm = c_ref[...]
    cb = lax.dot_general(cm.astype(BF16), bm.astype(BF16), (((1,), (1,)), ((), ())),
                         preferred_element_type=F32)
    bt = bm.T
    row = lax.broadcasted_iota(jnp.int32, (ch, ch), 0)
    col = lax.broadcasted_iota(jnp.int32, (ch, ch), 1)
    causal = col <= row
    first = lax.broadcasted_iota(jnp.int32, (ch, 2 * hp), 1) < hp
    dtr = dtr_ref[...]
    acsr = acsr_ref[...]
    acsc = acsc_ref[0]

    for pr in range(npairs):
        sl = slice(pr * 2 * hp, (pr + 1) * 2 * hp)
        xs_b = xs_ref[:, sl].astype(BF16)
        prev = st_ref[pr]
        rhs = jnp.concatenate([xs_b, prev.astype(BF16)], axis=0)
        outs, news = [], []
        for hh in (2 * pr, 2 * pr + 1):
            a_col = jnp.broadcast_to(acsc[:, hh:hh + 1], (ch, ch))
            a_row = acsr[hh:hh + 1, :]
            dt_row = dtr[hh:hh + 1, :]
            lmat = jnp.exp(jnp.where(causal, a_col - a_row, NEG_INF))
            scores = cb * lmat * dt_row
            c_dec = cm * jnp.exp(a_col)
            lhs = jnp.concatenate([scores.astype(BF16), c_dec.astype(BF16)], axis=1)
            outs.append(jnp.dot(lhs, rhs, preferred_element_type=F32))
            a_last = a_row[:, ch - 1:ch]
            b_dec = (bt * (dt_row * jnp.exp(a_last - a_row))).astype(BF16)
            news.append(jnp.exp(a_last) * prev
                        + jnp.dot(b_dec, xs_b, preferred_element_type=F32))
        y_ref[:, sl] = jnp.where(first, outs[0], outs[1])
        st_ref[pr] = jnp.where(first, news[0], news[1])

    y = y_ref[...] + dexp_ref[...] * xs_ref[...]
    z = z_ref[...]
    yg = y * (z * _sigmoid(z))
    o_ref[...] = (_rms(yg) * nw_ref[...]).astype(o_ref.dtype)


def _ssd_scan(xbc, z, dtr, acsr, acsc, d_exp, norm_w):
    l, d_inner = z.shape
    gw = d_inner // SSD_GROUPS
    npairs = gw // (2 * SSD_HEAD_DIM)
    hpg = gw // SSD_HEAD_DIM
    ch = SSD_CHUNK
    nb = d_inner // SSD_STATE
    return pl.pallas_call(
        functools.partial(_ssd_kernel, npairs=npairs),
        grid=(SSD_GROUPS, l // ch),
        in_specs=[
            pl.BlockSpec((ch, gw), lambda g, c: (c, g)),
            pl.BlockSpec((ch, SSD_STATE), lambda g, c: (c, nb + g)),
            pl.BlockSpec((ch, SSD_STATE), lambda g, c: (c, nb + SSD_GROUPS + g)),
            pl.BlockSpec((hpg, ch), lambda g, c: (g, c)),
            pl.BlockSpec((hpg, ch), lambda g, c: (g, c)),
            pl.BlockSpec((1, ch, hpg), lambda g, c: (g, c, 0)),
            pl.BlockSpec((ch, gw), lambda g, c: (c, g)),
            pl.BlockSpec((1, gw), lambda g, c: (0, g)),
            pl.BlockSpec((1, gw), lambda g, c: (0, g)),
        ],
        out_specs=pl.BlockSpec((ch, gw), lambda g, c: (c, g)),
        out_shape=jax.ShapeDtypeStruct((l, d_inner), BF16),
        scratch_shapes=[pltpu.VMEM((npairs, SSD_STATE, 2 * SSD_HEAD_DIM), F32),
                        pltpu.VMEM((ch, gw), F32)],
        compiler_params=_params(("arbitrary", "arbitrary")),
        name="ssd_scan",
    )(xbc, xbc, xbc, dtr, acsr, acsc, z, d_exp, norm_w.reshape(1, d_inner))


def _lookup(table, idx):
    hit = idx[:, None] == jnp.arange(table.shape[0], dtype=jnp.int32)[None, :]
    return jnp.sum(jnp.where(hit, table[None, :], 0), axis=1)


def _route_tables(route, ne, tr, tm):
    t = route.shape[0]
    a = t * TOP_K
    eidx = route[:, :TOP_K].astype(jnp.int32).reshape(a)
    order = jnp.argsort(eidx, stable=True).astype(jnp.int32)
    rank = jnp.argsort(order).astype(jnp.int32)
    counts = jnp.sum(eidx[:, None] == jnp.arange(ne, dtype=jnp.int32)[None, :], axis=0,
                     dtype=jnp.int32)
    starts = jnp.cumsum(counts) - counts
    padded = (counts + tr - 1) // tr * tr
    pends = jnp.cumsum(padded)
    pstarts = pends - padded
    pos = rank + _lookup(pstarts - starts, eidx)
    n_tiles = (a + ne * tr) // tr
    used = (pends[-1] // tr).astype(jnp.int32)
    tile0 = jnp.arange(n_tiles, dtype=jnp.int32) * tr
    tile0_used = jnp.minimum(tile0, (used - 1) * tr)
    tile_e = jnp.sum(pends[None, :] <= tile0_used[:, None], axis=1, dtype=jnp.int32)
    src_base = jnp.clip(_lookup(starts - pstarts, tile_e) + tile0_used, 0, a)
    order_tok = jnp.concatenate([order // TOP_K, jnp.zeros((tr,), jnp.int32)])
    pos_blk = pos.reshape(t // tm, tm, TOP_K).transpose(0, 2, 1).reshape(a)
    return order_tok, tile_e, src_base, used.reshape(1), pos_blk


def _moe_kernel(tile_e_ref, base_ref, used_ref, tok_ref, h_hbm, wgu_ref, wd_ref, bgu_ref, bd_ref,
                sel_ref, o_ref, buf, sem, *, tr):
    t = pl.program_id(0)
    used = used_ref[0]
    slot = t % 2
    f2 = bgu_ref.shape[-1]

    @pl.when(t == 0)
    def _():
        _gather_start(h_hbm, tok_ref, base_ref[0], tr, buf.at[0], sem.at[0])

    @pl.when(t < used)
    def _():
        _gather_wait(h_hbm, tr, buf.at[slot], sem.at[slot])
        nxt_base = base_ref[jnp.minimum(t + 1, used - 1)]
        for r in range(tr):
            _row_copy(h_hbm, tok_ref[nxt_base + r], buf.at[1 - slot], r, sem.at[1 - slot]).start()
        gu = jnp.dot(_unpack_bf16_pairs(buf[slot]).astype(BF16), wgu_ref[0, 0].astype(BF16),
                     preferred_element_type=F32) + bgu_ref[0]
        xg = jnp.minimum(gu, SWIGLU_LIMIT)
        glu = xg * _sigmoid(SWIGLU_ALPHA * xg)
        lin = jnp.clip(gu, -SWIGLU_LIMIT, SWIGLU_LIMIT) + 1.0
        pair = glu * pltpu.roll(lin, f2 - 1, 1)
        act = jnp.dot(pair.astype(BF16), sel_ref[...], preferred_element_type=F32)
        o_ref[...] = _pack_bf16_pairs(jnp.dot(act.astype(BF16), wd_ref[0, 0].astype(BF16),
                                              preferred_element_type=F32) + bd_ref[0])

    @pl.when(t + 1 == used)
    def _():
        _gather_wait(h_hbm, tr, buf.at[1 - slot], sem.at[1 - slot])

    @pl.when(t >= used)
    def _():
        o_ref[...] = jnp.zeros(o_ref.shape, o_ref.dtype)


def _moe_sparse(h, w_gu, w_down, layer, b_gu, b_down, order_tok, tile_e, src_base, used, tr):
    _, ne, d, f2 = w_gu.shape
    f = f2 // 2
    n_tiles = tile_e.shape[0]
    sel = (jnp.arange(f2)[:, None] == 2 * jnp.arange(f)[None, :]).astype(BF16)
    return pl.pallas_call(
        functools.partial(_moe_kernel, tr=tr),
        grid_spec=pltpu.PrefetchScalarGridSpec(
            num_scalar_prefetch=4,
            grid=(n_tiles,),
            in_specs=[
                pl.BlockSpec(memory_space=pl.ANY),
                pl.BlockSpec((1, 1, d, f2), lambda i, te, *_: (layer, te[i], 0, 0)),
                pl.BlockSpec((1, 1, f, d), lambda i, te, *_: (layer, te[i], 0, 0)),
                pl.BlockSpec((1, 1, f2), lambda i, te, *_: (te[i], 0, 0)),
                pl.BlockSpec((1, 1, d), lambda i, te, *_: (te[i], 0, 0)),
                pl.BlockSpec((f2, f), lambda i, *_: (0, 0)),
            ],
            out_specs=pl.BlockSpec((tr, d // 2), lambda i, *_: (i, 0)),
            scratch_shapes=[pltpu.VMEM((2, tr, d // 2), jnp.uint32),
                            pltpu.SemaphoreType.DMA((2,))],
        ),
        out_shape=jax.ShapeDtypeStruct((n_tiles * tr, d // 2), jnp.uint32),
        compiler_params=_params(("arbitrary",)),
        name="moe_experts",
    )(tile_e, src_base, used, order_tok, h, w_gu, w_down, b_gu, b_down, sel)


def kernel(x, c, w_cond, b_cond, w_mod, b_mod, g_mix_pre, g_mix_post, g_ffn_pre, g_ffn_post, attn_w_in, attn_w_out, ssd_w_in, ssd_conv_w, ssd_conv_b, ssd_dt_bias, ssd_a_log, ssd_d, ssd_norm_w, ssd_w_out, moe_w_router, moe_b_router, moe_w_gu, moe_b_gu, moe_w_down, moe_b_down):
    bsz, seq, d = x.shape
    assert bsz == 1, "kernel is written for a single sequence"
    depth = w_mod.shape[0]
    ne = moe_w_router.shape[-1]
    ff = moe_w_down.shape[2]
    d_inner = ssd_w_out.shape[1]
    conv_dim = ssd_conv_w.shape[-1]

    mod = _conditioning(c, w_cond, b_cond, w_mod, b_mod)

    def mods(i):
        return [mod[i, k * d:(k + 1) * d] for k in range(N_MOD)]

    xt = x.reshape(seq, d)
    tm = _tile(seq, GLUE_ROWS)
    sh_m, sc_m, ga_m, sh_f, sc_f, ga_f = mods(0)
    (h,) = _glue(xt, nxt=(g_mix_pre[0], sc_m, sh_m), tm=tm)

    for i in range(depth):
        j = i // 2
        if i % 2 == 0:
            q = _matmul(h, attn_w_in, j, BF16, col0=0, ncols=d,
                        scale=-LOG2E * SB_HEAD_DIM ** -0.5)
            kv = _matmul(h, attn_w_in, j, BF16, col0=d, ncols=2 * d)
            o = _attention(q, kv)
            y = _matmul(o, attn_w_out, j, F32)
        else:
            z = _matmul(h, ssd_w_in, j, F32, col0=0, ncols=d_inner)
            xbc = _matmul(h, ssd_w_in, j, F32, col0=d_inner, ncols=conv_dim)
            dt_raw = _matmul(h, ssd_w_in, j, F32, col0=d_inner + conv_dim)
            xbc = _conv_silu(xbc, ssd_conv_w[j], ssd_conv_b[j])
            dtr, acsr, acsc = _dtprep(dt_raw, ssd_dt_bias[j], ssd_a_log[j])
            d_exp = jnp.repeat(ssd_d[j], SSD_HEAD_DIM).reshape(1, d_inner)
            yn = _ssd_scan(xbc, z, dtr, acsr, acsc, d_exp, ssd_norm_w[j])
            y = _matmul(yn, ssd_w_out, j, F32)

        wr = jnp.pad(moe_w_router[i], ((0, 0), (0, LANES - ne)))
        wr_hi = wr.astype(BF16)
        wr = jnp.stack([wr_hi, (wr - wr_hi.astype(F32)).astype(BF16)])
        br = jnp.pad(moe_b_router[i], (0, LANES - ne), constant_values=NEG_INF).reshape(1, LANES)
        xt, h, route = _glue(xt, resid=(y, ga_m, g_mix_post[i]),
                             nxt=(g_ffn_pre[i], sc_f, sh_f), router=(wr, br), tm=tm)

        order_tok, tile_e, src_base, used, pos_blk = _route_tables(route, ne, MOE_TILE_ROWS, tm)
        y_rows = _moe_sparse(h, moe_w_gu, moe_w_down, i, moe_b_gu[i].reshape(ne, 1, 2 * ff),
                             moe_b_down[i].reshape(ne, 1, d), order_tok, tile_e, src_base, used,
                             MOE_TILE_ROWS)

        ga_prev = ga_f
        if i + 1 < depth:
            sh_m, sc_m, ga_m, sh_f, sc_f, ga_f = mods(i + 1)
            xt, h = _glue(xt, resid=(y_rows, ga_prev, g_ffn_post[i]),
                          nxt=(g_mix_pre[i + 1], sc_m, sh_m), gather=(pos_blk, route), tm=tm)
        else:
            (xt,) = _glue(xt, resid=(y_rows, ga_prev, g_ffn_post[i]), gather=(pos_blk, route),
                          tm=tm)

    return xt.reshape(bsz, seq, d)
```

```python
import functools
import math

import jax
import jax.numpy as jnp
from jax import lax
from jax.experimental import pallas as pl
from jax.experimental.pallas import tpu as pltpu

SB_HEAD_DIM = 128
SSD_HEAD_DIM = 64
SSD_GROUPS = 8
SSD_STATE = 128
SSD_CONV = 4
SSD_CHUNK = 128
TOP_K = 4
SWIGLU_ALPHA = 1.702
SWIGLU_LIMIT = 7.0
N_MOD = 6
NORM_EPS = 1e-6

LANES = 128
SUBLANES = 8
VMEM_LIMIT_BYTES = 56 * 1024 * 1024

GLUE_ROWS = 128
MOE_TILE_ROWS = 256
GATHER_UNROLL = 8

F32 = jnp.float32
BF16 = jnp.bfloat16
HIGHEST = lax.Precision.HIGHEST
NEG_INF = float("-inf")
LOG2E = math.log2(math.e)
F32_UNDERFLOW_LOG2 = -160.0


def _params(sem):
    return pltpu.CompilerParams(dimension_semantics=sem, vmem_limit_bytes=VMEM_LIMIT_BYTES)


def _sigmoid(x):
    return 1.0 / (1.0 + jnp.exp(-x))


def _softplus(x):
    return jnp.maximum(x, 0.0) + jnp.log(1.0 + jnp.exp(-jnp.abs(x)))


def _tile(n, pref):
    if n <= pref:
        return n
    t = pref
    while n % t:
        t //= 2
    return t


def _cond_kernel(c_ref, w_ref, b_ref, o_ref):
    c = c_ref[...]
    s = c * _sigmoid(c)
    r = jnp.dot(s, w_ref[...], precision=HIGHEST, preferred_element_type=F32) + b_ref[...]
    o_ref[...] = r * _sigmoid(r)


def _mod_kernel(cond_ref, w_ref, b_ref, o_ref):
    o_ref[0] = jnp.dot(cond_ref[...], w_ref[0], precision=HIGHEST,
                       preferred_element_type=F32) + b_ref[0]


def _conditioning(c, w_cond, b_cond, w_mod, b_mod):
    bsz, d = c.shape
    rank = w_cond.shape[1]
    depth, _, nmod = w_mod.shape
    rows = SUBLANES
    c8 = jnp.broadcast_to(c[:1], (rows, d))
    cond = pl.pallas_call(
        _cond_kernel,
        out_shape=jax.ShapeDtypeStruct((rows, rank), F32),
        compiler_params=_params(None),
        name="cond_trunk",
    )(c8, w_cond, b_cond.reshape(1, rank))
    tn = _tile(nmod, 4096)
    mod = pl.pallas_call(
        _mod_kernel,
        grid=(depth, nmod // tn),
        in_specs=[
            pl.BlockSpec((rows, rank), lambda i, j: (0, 0)),
            pl.BlockSpec((1, rank, tn), lambda i, j: (i, 0, j)),
            pl.BlockSpec((1, 1, tn), lambda i, j: (i, 0, j)),
        ],
        out_specs=pl.BlockSpec((1, rows, tn), lambda i, j: (i, 0, j)),
        out_shape=jax.ShapeDtypeStruct((depth, rows, nmod), F32),
        compiler_params=_params(("arbitrary", "arbitrary")),
        name="cond_mod",
    )(cond, w_mod, b_mod.reshape(depth, 1, nmod))
    return mod[:, 0, :]


def _rms(y):
    return y * lax.rsqrt(jnp.mean(y * y, axis=-1, keepdims=True) + NORM_EPS)


def _pack_bf16_pairs(v):
    half = v.shape[1] // 2
    bits = lax.bitcast_convert_type(v.astype(BF16).astype(F32), jnp.uint32)
    return (bits[:, :half] >> 16) | bits[:, half:]


def _unpack_bf16_pairs(w):
    lo = lax.bitcast_convert_type(w << 16, F32)
    hi = lax.bitcast_convert_type(w & jnp.uint32(0xFFFF0000), F32)
    return jnp.concatenate([lo, hi], axis=1)


def _row_copy(src_hbm, row, dst, r, sem):
    return pltpu.make_async_copy(src_hbm.at[pl.ds(row, 1)], dst.at[pl.ds(r, 1)], sem)


def _gather_start(src_hbm, idx_ref, base, n, dst, sem):
    def body(r, carry):
        _row_copy(src_hbm, idx_ref[base + r], dst, r, sem).start()
        return carry

    lax.fori_loop(0, n, body, 0, unroll=GATHER_UNROLL)


def _gather_wait(src_hbm, n, dst, sem):
    def body(r, carry):
        _row_copy(src_hbm, 0, dst, r, sem).wait()
        return carry

    lax.fori_loop(0, n, body, 0, unroll=GATHER_UNROLL)


def _glue_kernel(*refs, has_resid, has_next, has_router, gathered):
    it = iter(refs)
    if gathered:
        pos_ref = next(it)
    x_ref = next(it)
    if gathered:
        rw_ref = next(it)
    if has_resid:
        y_ref, ga_ref, gpost_ref = next(it), next(it), next(it)
    if has_next:
        gpre_ref, sc_ref, sh_ref = next(it), next(it), next(it)
    if has_router:
        wr_ref, br_ref = next(it), next(it)
    if has_resid:
        xo_ref = next(it)
    if has_next:
        h_ref = next(it)
    if has_router:
        route_ref = next(it)
    if gathered:
        buf, sem = next(it), next(it)

    x = x_ref[...]
    if has_resid:
        if gathered:
            i = pl.program_id(0)
            tm = x.shape[0]
            nrow = TOP_K * tm
            slot = i % 2

            @pl.when(i == 0)
            def _():
                _gather_start(y_ref, pos_ref, 0, nrow, buf.at[0], sem.at[0])

            _gather_wait(y_ref, nrow, buf.at[slot], sem.at[slot])
            last = pl.num_programs(0) - 1
            nxt_base = jnp.minimum(i + 1, last) * nrow
            for r in range(nrow):
                _row_copy(y_ref, pos_ref[nxt_base + r], buf.at[1 - slot], r,
                          sem.at[1 - slot]).start()
            rw = rw_ref[...]
            y = rw[:, TOP_K:TOP_K + 1] * _unpack_bf16_pairs(buf[slot, 0:tm, :])
            for k in range(1, TOP_K):
                y = y + (rw[:, TOP_K + k:TOP_K + k + 1]
                         * _unpack_bf16_pairs(buf[slot, k * tm:(k + 1) * tm, :]))
        else:
            y = y_ref[...]
        x = x + ga_ref[...] * (_rms(y) * gpost_ref[...])
        xo_ref[...] = x
    if has_next:
        h = (_rms(x) * gpre_ref[...]) * (1.0 + sc_ref[...]) + sh_ref[...]
        if has_router:
            h_ref[...] = _pack_bf16_pairs(h)
        else:
            h_ref[...] = h.astype(h_ref.dtype)
    if has_router:
        h_hi = h.astype(BF16)
        h_lo = (h - h_hi.astype(F32)).astype(BF16)
        w_hi, w_lo = wr_ref[0], wr_ref[1]
        logits = (jnp.dot(h_hi, w_hi, preferred_element_type=F32)
                  + (jnp.dot(h_lo, w_hi, preferred_element_type=F32)
                     + jnp.dot(h_hi, w_lo, preferred_element_type=F32))) + br_ref[...]
        lane = lax.broadcasted_iota(jnp.int32, logits.shape, 1).astype(F32)
        work = logits
        vals, idxs = [], []
        for _ in range(TOP_K):
            m = jnp.max(work, axis=1, keepdims=True)
            idx = jnp.min(jnp.where(work == m, lane, float(LANES)), axis=1, keepdims=True)
            vals.append(m)
            idxs.append(idx)
            work = jnp.where(lane == idx, NEG_INF, work)
        exps = [jnp.exp(v - vals[0]) for v in vals]
        denom = exps[0]
        for e in exps[1:]:
            denom = denom + e
        route = jnp.zeros_like(logits)
        for j in range(TOP_K):
            route = jnp.where(lane == float(j), idxs[j], route)
            route = jnp.where(lane == float(TOP_K + j), exps[j] / denom, route)
        route_ref[...] = route
    if gathered:
        @pl.when(i == last)
        def _():
            _gather_wait(y_ref, nrow, buf.at[1 - slot], sem.at[1 - slot])


def _glue(x, resid=None, nxt=None, router=None, gather=None, tm=GLUE_ROWS):
    t, d = x.shape
    tm = _tile(t, tm)
    gathered = gather is not None
    row = pl.BlockSpec((tm, d), lambda i, *_: (i, 0))
    vec = pl.BlockSpec((1, d), lambda i, *_: (0, 0))
    lanes = pl.BlockSpec((tm, LANES), lambda i, *_: (i, 0))
    args, in_specs, out_shape, out_specs, scratch = [x], [row], [], [], []
    if gathered:
        args.append(gather[1])
        in_specs.append(lanes)
    if resid is not None:
        y, ga, gpost = resid
        args += [y, ga.reshape(1, d), gpost.reshape(1, d)]
        in_specs += [pl.BlockSpec(memory_space=pl.ANY) if gathered else row, vec, vec]
        out_shape.append(jax.ShapeDtypeStruct((t, d), F32))
        out_specs.append(row)
    if nxt is not None:
        gpre, sc, sh = nxt
        args += [gpre.reshape(1, d), sc.reshape(1, d), sh.reshape(1, d)]
        in_specs += [vec, vec, vec]
        if router is None:
            out_shape.append(jax.ShapeDtypeStruct((t, d), BF16))
            out_specs.append(row)
        else:
            out_shape.append(jax.ShapeDtypeStruct((t, d // 2), jnp.uint32))
            out_specs.append(pl.BlockSpec((tm, d // 2), lambda i, *_: (i, 0)))
    if router is not None:
        wr, br = router
        args += [wr, br]
        in_specs += [pl.BlockSpec((2, d, LANES), lambda i, *_: (0, 0, 0)),
                     pl.BlockSpec((1, LANES), lambda i, *_: (0, 0))]
        out_shape.append(jax.ShapeDtypeStruct((t, LANES), F32))
        out_specs.append(lanes)
    if gathered:
        args = [gather[0]] + args
        scratch = [pltpu.VMEM((2, TOP_K * tm, d // 2), jnp.uint32),
                   pltpu.SemaphoreType.DMA((2,))]
    outs = pl.pallas_call(
        functools.partial(_glue_kernel, has_resid=resid is not None, has_next=nxt is not None,
                          has_router=router is not None, gathered=gathered),
        grid_spec=pltpu.PrefetchScalarGridSpec(
            num_scalar_prefetch=1 if gathered else 0,
            grid=(t // tm,),
            in_specs=in_specs,
            out_specs=out_specs,
            scratch_shapes=scratch,
        ),
        out_shape=out_shape,
        compiler_params=_params(("arbitrary",)),
        name="glue_gather" if gathered else "glue",
    )(*args)
    return outs


def _mm_kernel(a_ref, w_ref, o_ref, *scratch, nk, scale):
    part = jnp.dot(a_ref[...], w_ref[0].astype(BF16), preferred_element_type=F32)
    if nk == 1:
        if scale != 1.0:
            part = part * scale
        o_ref[...] = part.astype(o_ref.dtype)
        return
    acc_ref = scratch[0] if scratch else o_ref
    k = pl.program_id(2)

    @pl.when(k == 0)
    def _():
        acc_ref[...] = part

    @pl.when(k > 0)
    def _():
        acc_ref[...] += part

    if scratch or scale != 1.0:
        @pl.when(k == nk - 1)
        def _():
            r = acc_ref[...]
            if scale != 1.0:
                r = r * scale
            o_ref[...] = r.astype(o_ref.dtype)


def _matmul(a, w, layer, out_dtype, col0=0, ncols=None, scale=1.0, tm=1024, tn=512, tk=4096):
    m, k = a.shape
    n = w.shape[2] - col0 if ncols is None else ncols
    tm, tk = _tile(m, tm), _tile(k, tk)
    nk = k // tk
    wide = nk == 1 and n % (2 * tn) == 0 and col0 % (2 * tn) == 0
    tn = _tile(n, 2 * tn if wide else tn)
    assert col0 % tn == 0 and n % tn == 0
    jb = col0 // tn
    scratch = []
    if nk > 1 and out_dtype != F32:
        scratch = [pltpu.VMEM((tm, tn), F32)]
    a_mode = dict(pipeline_mode=pl.Buffered(1)) if wide else {}
    return pl.pallas_call(
        functools.partial(_mm_kernel, nk=nk, scale=scale),
        grid=(m // tm, n // tn, nk),
        in_specs=[pl.BlockSpec((tm, tk), lambda i, j, kk: (i, kk), **a_mode),
                  pl.BlockSpec((1, tk, tn), lambda i, j, kk: (layer, kk, jb + j))],
        out_specs=pl.BlockSpec((tm, tn), lambda i, j, kk: (i, j)),
        out_shape=jax.ShapeDtypeStruct((m, n), out_dtype),
        scratch_shapes=scratch,
        compiler_params=_params(("arbitrary", "arbitrary", "arbitrary")),
        name="matmul",
    )(a, w)


def _attn_kernel(q_ref, k_ref, v_ref, o_ref, acc_ref, carry_ref, *, tq, tk):
    i = pl.program_id(1)
    r2 = lax.broadcasted_iota(jnp.int32, (tk, tk), 0)
    c2 = lax.broadcasted_iota(jnp.int32, (tk, tk), 1)
    later = jnp.where(r2 > c2, 1.0, 0.0).astype(BF16)

    acc_ref[...] = jnp.zeros(acc_ref.shape, F32)
    carry_ref[...] = jnp.zeros(carry_ref.shape, F32)

    def span(k_start, nsub, r0, r1, diag):
        kk = k_ref[pl.ds(k_start, nsub * tk), :]
        vv = v_ref[pl.ds(k_start, nsub * tk), :]
        n = lax.dot_general(q_ref[r0:r1, :], kk, (((1,), (1,)), ((), ())),
                            preferred_element_type=F32)
        neg_abs = lax.bitcast_convert_type(
            lax.bitcast_convert_type(n, jnp.uint32) | jnp.uint32(0x80000000), F32)
        lk = jnp.minimum(n, 0.0) - jnp.log(1.0 + jnp.exp2(neg_abs)) * LOG2E
        if diag:
            row = lax.broadcasted_iota(jnp.int32, n.shape, 0)
            col = lax.broadcasted_iota(jnp.int32, n.shape, 1)
            vis = col < row
            lk = jnp.where(vis, lk, 0.0)
        carry = carry_ref[r0:r1, :]
        ws = [None] * nsub
        for s in range(nsub - 1, -1, -1):
            lk_s = lk[:, s * tk:(s + 1) * tk]
            la = jnp.dot(lk_s.astype(BF16), later, preferred_element_type=F32) + carry
            ws[s] = jnp.exp2((la + lk_s) - n[:, s * tk:(s + 1) * tk])
            carry = la[:, 0:1] + lk_s[:, 0:1]
        w = ws[0] if nsub == 1 else jnp.concatenate(ws, axis=1)
        if diag:
            w = jnp.where(vis, w, 0.0)
        acc_ref[r0:r1, :] += jnp.dot(w.astype(BF16), vv, preferred_element_type=F32)
        carry_ref[r0:r1, :] = carry

    q0 = i * tq
    for d in range(tq // tk - 1, -1, -1):
        span(pl.multiple_of(q0 + d * tk, tk), 1, d * tk, tq, True)

    nsub = 1
    nstep = q0 // (nsub * tk)

    def more(state):
        t, top = state
        return jnp.logical_and(t < nstep, top > F32_UNDERFLOW_LOG2)

    def body(state):
        t, _ = state
        span(pl.multiple_of((nstep - 1 - t) * (nsub * tk), nsub * tk), nsub, 0, tq, False)
        return t + 1, jnp.max(carry_ref[...])

    lax.while_loop(more, body, (jnp.int32(0), jnp.max(carry_ref[...])))
    o_ref[...] = acc_ref[...].astype(o_ref.dtype)


def _attention(q, kv, tq=1024, tk=256):
    l, d = q.shape
    nh = d // SB_HEAD_DIM
    tq = _tile(l, tq)
    tk = _tile(tq, tk)
    return pl.pallas_call(
        functools.partial(_attn_kernel, tq=tq, tk=tk),
        grid=(nh, l // tq),
        in_specs=[
            pl.BlockSpec((tq, SB_HEAD_DIM), lambda h, i: (i, h)),
            pl.BlockSpec((l, SB_HEAD_DIM), lambda h, i: (0, h)),
            pl.BlockSpec((l, SB_HEAD_DIM), lambda h, i: (0, nh + h)),
        ],
        out_specs=pl.BlockSpec((tq, SB_HEAD_DIM), lambda h, i: (i, h)),
        out_shape=jax.ShapeDtypeStruct((l, d), BF16),
        scratch_shapes=[pltpu.VMEM((tq, SB_HEAD_DIM), F32), pltpu.VMEM((tq, 1), F32)],
        compiler_params=_params(("arbitrary", "arbitrary")),
        name="sb_attention",
    )(q, kv, kv)


def _conv_kernel(u_ref, w_ref, b_ref, o_ref, ext_ref, *, tl):
    halo = SUBLANES

    @pl.when(pl.program_id(1) == 0)
    def _():
        ext_ref[0:halo, :] = jnp.zeros((halo, ext_ref.shape[1]), F32)

    u = u_ref[...]
    ext_ref[halo:halo + tl, :] = u
    acc = b_ref[...] + w_ref[SSD_CONV - 1:SSD_CONV, :] * u
    for k in range(SSD_CONV - 1):
        off = halo - (SSD_CONV - 1) + k
        acc = acc + w_ref[k:k + 1, :] * ext_ref[off:off + tl, :]
    o_ref[...] = acc * _sigmoid(acc)
    ext_ref[0:halo, :] = u[tl - halo:tl, :]


def _conv_silu(u, w, b, tl=512, tc=1024):
    l, c = u.shape
    tl, tc = _tile(l, tl), _tile(c, tc)
    return pl.pallas_call(
        functools.partial(_conv_kernel, tl=tl),
        grid=(c // tc, l // tl),
        in_specs=[
            pl.BlockSpec((tl, tc), lambda j, i: (i, j)),
            pl.BlockSpec((SSD_CONV, tc), lambda j, i: (0, j)),
            pl.BlockSpec((1, tc), lambda j, i: (0, j)),
        ],
        out_specs=pl.BlockSpec((tl, tc), lambda j, i: (i, j)),
        out_shape=jax.ShapeDtypeStruct((l, c), F32),
        scratch_shapes=[pltpu.VMEM((tl + SUBLANES, tc), F32)],
        compiler_params=_params(("arbitrary", "arbitrary")),
        name="ssd_conv",
    )(u, w, b.reshape(1, c))


def _split3(x):
    p1 = x.astype(BF16)
    r1 = x - p1.astype(F32)
    p2 = r1.astype(BF16)
    p3 = (r1 - p2.astype(F32)).astype(BF16)
    return p1, p2, p3


def _dtprep_kernel(dt_ref, bias_ref, alog_ref, dtr_ref, acsr_ref, acsc_ref, *, hpg):
    dt = _softplus(dt_ref[...] + bias_ref[...])
    adt = dt * (-jnp.exp(alog_ref[...]))
    n = dt.shape[0]
    r = lax.broadcasted_iota(jnp.int32, (n, n), 0)
    c = lax.broadcasted_iota(jnp.int32, (n, n), 1)
    upto = jnp.where(c <= r, 1.0, 0.0).astype(BF16)
    acs = sum(jnp.dot(upto, p, preferred_element_type=F32) for p in _split3(adt))
    dtr_ref[...] = dt.T
    acsr_ref[...] = acs.T
    for g in range(acsc_ref.shape[0]):
        acsc_ref[g] = acs[:, g * hpg:(g + 1) * hpg]


def _dtprep(dt_raw, dt_bias, a_log):
    l, nh = dt_raw.shape
    hpg = nh // SSD_GROUPS
    ch = SSD_CHUNK
    return pl.pallas_call(
        functools.partial(_dtprep_kernel, hpg=hpg),
        grid=(l // ch,),
        in_specs=[
            pl.BlockSpec((ch, nh), lambda i: (i, 0)),
            pl.BlockSpec((1, nh), lambda i: (0, 0)),
            pl.BlockSpec((1, nh), lambda i: (0, 0)),
        ],
        out_specs=[
            pl.BlockSpec((nh, ch), lambda i: (0, i)),
            pl.BlockSpec((nh, ch), lambda i: (0, i)),
            pl.BlockSpec((SSD_GROUPS, ch, hpg), lambda i: (0, i, 0)),
        ],
        out_shape=[
            jax.ShapeDtypeStruct((nh, l), F32),
            jax.ShapeDtypeStruct((nh, l), F32),
            jax.ShapeDtypeStruct((SSD_GROUPS, l, hpg), F32),
        ],
        compiler_params=_params(("arbitrary",)),
        name="ssd_dtprep",
    )(dt_raw, dt_bias.reshape(1, nh), a_log.reshape(1, nh))


def _ssd_kernel(xs_ref, b_ref, c_ref, dtr_ref, acsr_ref, acsc_ref, z_ref, dexp_ref, nw_ref,
                o_ref, st_ref, y_ref, *, npairs):
    ch = SSD_CHUNK
    hp = SSD_HEAD_DIM

    @pl.when(pl.program_id(1) == 0)
    def _():
        st_ref[...] = jnp.zeros(st_ref.shape, F32)

    bm = b_ref[...]
    cm = c_ref[...]
    cb = lax.dot_general(cm.astype(BF16), bm.astype(BF16), (((1,), (1,)), ((), ())),
                         preferred_element_type=F32)
    bt = bm.T
    row = lax.broadcasted_iota(jnp.int32, (ch, ch), 0)
    col = lax.broadcasted_iota(jnp.int32, (ch, ch), 1)
    causal = col <= row
    first = lax.broadcasted_iota(jnp.int32, (ch, 2 * hp), 1) < hp
    dtr = dtr_ref[...]
    acsr = acsr_ref[...]
    acsc = acsc_ref[0]

    for pr in range(npairs):
        sl = slice(pr * 2 * hp, (pr + 1) * 2 * hp)
        xs_b = xs_ref[:, sl].astype(BF16)
        prev = st_ref[pr]
        rhs = jnp.concatenate([xs_b, prev.astype(BF16)], axis=0)
        outs, news = [], []
        for hh in (2 * pr, 2 * pr + 1):
            a_col = jnp.broadcast_to(acsc[:, hh:hh + 1], (ch, ch))
            a_row = acsr[hh:hh + 1, :]
            dt_row = dtr[hh:hh + 1, :]
            lmat = jnp.exp(jnp.where(causal, a_col - a_row, NEG_INF))
            scores = cb * lmat * dt_row
            c_dec = cm * jnp.exp(a_col)
            lhs = jnp.concatenate([scores.astype(BF16), c_dec.astype(BF16)], axis=1)
            outs.append(jnp.dot(lhs, rhs, preferred_element_type=F32))
            a_last = a_row[:, ch - 1:ch]
            b_dec = (bt * (dt_row * jnp.exp(a_last - a_row))).astype(BF16)
            news.append(jnp.exp(a_last) * prev
                        + jnp.dot(b_dec, xs_b, preferred_element_type=F32))
        y_ref[:, sl] = jnp.where(first, outs[0], outs[1])
        st_ref[pr] = jnp.where(first, news[0], news[1])

    y = y_ref[...] + dexp_ref[...] * xs_ref[...]
    z = z_ref[...]
    yg = y * (z * _sigmoid(z))
    o_ref[...] = (_rms(yg) * nw_ref[...]).astype(o_ref.dtype)


def _ssd_scan(xbc, z, dtr, acsr, acsc, d_exp, norm_w):
    l, d_inner = z.shape
    gw = d_inner // SSD_GROUPS
    npairs = gw // (2 * SSD_HEAD_DIM)
    hpg = gw // SSD_HEAD_DIM
    ch = SSD_CHUNK
    nb = d_inner // SSD_STATE
    return pl.pallas_call(
        functools.partial(_ssd_kernel, npairs=npairs),
        grid=(SSD_GROUPS, l // ch),
        in_specs=[
            pl.BlockSpec((ch, gw), lambda g, c: (c, g)),
            pl.BlockSpec((ch, SSD_STATE), lambda g, c: (c, nb + g)),
            pl.BlockSpec((ch, SSD_STATE), lambda g, c: (c, nb + SSD_GROUPS + g)),
            pl.BlockSpec((hpg, ch), lambda g, c: (g, c)),
            pl.BlockSpec((hpg, ch), lambda g, c: (g, c)),
            pl.BlockSpec((1, ch, hpg), lambda g, c: (g, c, 0)),
            pl.BlockSpec((ch, gw), lambda g, c: (c, g)),
            pl.BlockSpec((1, gw), lambda g, c: (0, g)),
            pl.BlockSpec((1, gw), lambda g, c: (0, g)),
        ],
        out_specs=pl.BlockSpec((ch, gw), lambda g, c: (c, g)),
        out_shape=jax.ShapeDtypeStruct((l, d_inner), BF16),
        scratch_shapes=[pltpu.VMEM((npairs, SSD_STATE, 2 * SSD_HEAD_DIM), F32),
                        pltpu.VMEM((ch, gw), F32)],
        compiler_params=_params(("arbitrary", "arbitrary")),
        name="ssd_scan",
    )(xbc, xbc, xbc, dtr, acsr, acsc, z, d_exp, norm_w.reshape(1, d_inner))


def _lookup(table, idx):
    hit = idx[:, None] == jnp.arange(table.shape[0], dtype=jnp.int32)[None, :]
    return jnp.sum(jnp.where(hit, table[None, :], 0), axis=1)


def _route_tables(route, ne, tr, tm):
    t = route.shape[0]
    a = t * TOP_K
    eidx = route[:, :TOP_K].astype(jnp.int32).reshape(a)
    order = jnp.argsort(eidx, stable=True).astype(jnp.int32)
    rank = jnp.argsort(order).astype(jnp.int32)
    counts = jnp.sum(eidx[:, None] == jnp.arange(ne, dtype=jnp.int32)[None, :], axis=0,
                     dtype=jnp.int32)
    starts = jnp.cumsum(counts) - counts
    padded = (counts + tr - 1) // tr * tr
    pends = jnp.cumsum(padded)
    pstarts = pends - padded
    pos = rank + _lookup(pstarts - starts, eidx)
    n_tiles = (a + ne * tr) // tr
    used = (pends[-1] // tr).astype(jnp.int32)
    tile0 = jnp.arange(n_tiles, dtype=jnp.int32) * tr
    tile0_used = jnp.minimum(tile0, (used - 1) * tr)
    tile_e = jnp.sum(pends[None, :] <= tile0_used[:, None], axis=1, dtype=jnp.int32)
    src_base = jnp.clip(_lookup(starts - pstarts, tile_e) + tile0_used, 0, a)
    order_tok = jnp.concatenate([order // TOP_K, jnp.zeros((tr,), jnp.int32)])
    pos_blk = pos.reshape(t // tm, tm, TOP_K).transpose(0, 2, 1).reshape(a)
    return order_tok, tile_e, src_base, used.reshape(1), pos_blk


def _moe_kernel(tile_e_ref, base_ref, used_ref, tok_ref, h_hbm, wgu_ref, wd_ref, bgu_ref, bd_ref,
                sel_ref, o_ref, buf, sem, *, tr):
    t = pl.program_id(0)
    used = used_ref[0]
    slot = t % 2
    f2 = bgu_ref.shape[-1]

    @pl.when(t == 0)
    def _():
        _gather_start(h_hbm, tok_ref, base_ref[0], tr, buf.at[0], sem.at[0])

    @pl.when(t < used)
    def _():
        _gather_wait(h_hbm, tr, buf.at[slot], sem.at[slot])
        nxt_base = base_ref[jnp.minimum(t + 1, used - 1)]
        for r in range(tr):
            _row_copy(h_hbm, tok_ref[nxt_base + r], buf.at[1 - slot], r, sem.at[1 - slot]).start()
        gu = jnp.dot(_unpack_bf16_pairs(buf[slot]).astype(BF16), wgu_ref[0, 0].astype(BF16),
                     preferred_element_type=F32) + bgu_ref[0]
        xg = jnp.minimum(gu, SWIGLU_LIMIT)
        glu = xg * _sigmoid(SWIGLU_ALPHA * xg)
        lin = jnp.clip(gu, -SWIGLU_LIMIT, SWIGLU_LIMIT) + 1.0
        pair = glu * pltpu.roll(lin, f2 - 1, 1)
        act = jnp.dot(pair.astype(BF16), sel_ref[...], preferred_element_type=F32)
        o_ref[...] = _pack_bf16_pairs(jnp.dot(act.astype(BF16), wd_ref[0, 0].astype(BF16),
                                              preferred_element_type=F32) + bd_ref[0])

    @pl.when(t + 1 == used)
    def _():
        _gather_wait(h_hbm, tr, buf.at[1 - slot], sem.at[1 - slot])

    @pl.when(t >= used)
    def _():
        o_ref[...] = jnp.zeros(o_ref.shape, o_ref.dtype)


def _moe_sparse(h, w_gu, w_down, layer, b_gu, b_down, order_tok, tile_e, src_base, used, tr):
    _, ne, d, f2 = w_gu.shape
    f = f2 // 2
    n_tiles = tile_e.shape[0]
    sel = (jnp.arange(f2)[:, None] == 2 * jnp.arange(f)[None, :]).astype(BF16)
    return pl.pallas_call(
        functools.partial(_moe_kernel, tr=tr),
        grid_spec=pltpu.PrefetchScalarGridSpec(
            num_scalar_prefetch=4,
            grid=(n_tiles,),
            in_specs=[
                pl.BlockSpec(memory_space=pl.ANY),
                pl.BlockSpec((1, 1, d, f2), lambda i, te, *_: (layer, te[i], 0, 0)),
                pl.BlockSpec((1, 1, f, d), lambda i, te, *_: (layer, te[i], 0, 0)),
                pl.BlockSpec((1, 1, f2), lambda i, te, *_: (te[i], 0, 0)),
                pl.BlockSpec((1, 1, d), lambda i, te, *_: (te[i], 0, 0)),
                pl.BlockSpec((f2, f), lambda i, *_: (0, 0)),
            ],
            out_specs=pl.BlockSpec((tr, d // 2), lambda i, *_: (i, 0)),
            scratch_shapes=[pltpu.VMEM((2, tr, d // 2), jnp.uint32),
                            pltpu.SemaphoreType.DMA((2,))],
        ),
        out_shape=jax.ShapeDtypeStruct((n_tiles * tr, d // 2), jnp.uint32),
        compiler_params=_params(("arbitrary",)),
        name="moe_experts",
    )(tile_e, src_base, used, order_tok, h, w_gu, w_down, b_gu, b_down, sel)


def kernel(x, c, w_cond, b_cond, w_mod, b_mod, g_mix_pre, g_mix_post, g_ffn_pre, g_ffn_post, attn_w_in, attn_w_out, ssd_w_in, ssd_conv_w, ssd_conv_b, ssd_dt_bias, ssd_a_log, ssd_d, ssd_norm_w, ssd_w_out, moe_w_router, moe_b_router, moe_w_gu, moe_b_gu, moe_w_down, moe_b_down):
    bsz, seq, d = x.shape
    assert bsz == 1, "kernel is written for a single sequence"
    depth = w_mod.shape[0]
    ne = moe_w_router.shape[-1]
    ff = moe_w_down.shape[2]
    d_inner = ssd_w_out.shape[1]
    conv_dim = ssd_conv_w.shape[-1]

    mod = _conditioning(c, w_cond, b_cond, w_mod, b_mod)

    def mods(i):
        return [mod[i, k * d:(k + 1) * d] for k in range(N_MOD)]

    xt = x.reshape(seq, d)
    tm = _tile(seq, GLUE_ROWS)
    sh_m, sc_m, ga_m, sh_f, sc_f, ga_f = mods(0)
    (h,) = _glue(xt, nxt=(g_mix_pre[0], sc_m, sh_m), tm=tm)

    for i in range(depth):
        j = i // 2
        if i % 2 == 0:
            q = _matmul(h, attn_w_in, j, BF16, col0=0, ncols=d,
                        scale=-LOG2E * SB_HEAD_DIM ** -0.5)
            kv = _matmul(h, attn_w_in, j, BF16, col0=d, ncols=2 * d)
            o = _attention(q, kv)
            y = _matmul(o, attn_w_out, j, F32)
        else:
            z = _matmul(h, ssd_w_in, j, F32, col0=0, ncols=d_inner)
            xbc = _matmul(h, ssd_w_in, j, F32, col0=d_inner, ncols=conv_dim)
            dt_raw = _matmul(h, ssd_w_in, j, F32, col0=d_inner + conv_dim)
            xbc = _conv_silu(xbc, ssd_conv_w[j], ssd_conv_b[j])
            dtr, acsr, acsc = _dtprep(dt_raw, ssd_dt_bias[j], ssd_a_log[j])
            d_exp = jnp.repeat(ssd_d[j], SSD_HEAD_DIM).reshape(1, d_inner)
            yn = _ssd_scan(xbc, z, dtr, acsr, acsc, d_exp, ssd_norm_w[j])
            y = _matmul(yn, ssd_w_out, j, F32)

        wr = jnp.pad(moe_w_router[i], ((0, 0), (0, LANES - ne)))
        wr_hi = wr.astype(BF16)
        wr = jnp.stack([wr_hi, (wr - wr_hi.astype(F32)).astype(BF16)])
        br = jnp.pad(moe_b_router[i], (0, LANES - ne), constant_values=NEG_INF).reshape(1, LANES)
        xt, h, route = _glue(xt, resid=(y, ga_m, g_mix_post[i]),
                             nxt=(g_ffn_pre[i], sc_f, sh_f), router=(wr, br), tm=tm)

        order_tok, tile_e, src_base, used, pos_blk = _route_tables(route, ne, MOE_TILE_ROWS, tm)
        y_rows = _moe_sparse(h, moe_w_gu, moe_w_down, i, moe_b_gu[i].reshape(ne, 1, 2 * ff),
                             moe_b_down[i].reshape(ne, 1, d), order_tok, tile_e, src_base, used,
                             MOE_TILE_ROWS)

        ga_prev = ga_f
        if i + 1 < depth:
            sh_m, sc_m, ga_m, sh_f, sc_f, ga_f = mods(i + 1)
            xt, h = _glue(xt, resid=(y_rows, ga_prev, g_ffn_post[i]),
                          nxt=(g_mix_pre[i + 1], sc_m, sh_m), gather=(pos_blk, route), tm=tm)
        else:
            (xt,) = _glue(xt, resid=(y_rows, ga_prev, g_ffn_post[i]), gather=(pos_blk, route),
                          tm=tm)

    return xt.reshape(bsz, seq, d)
```

```python
import functools
import math

import jax
import jax.numpy as jnp
from jax import lax
from jax.experimental import pallas as pl
from jax.experimental.pallas import tpu as pltpu

SB_HEAD_DIM = 128
SSD_HEAD_DIM = 64
SSD_GROUPS = 8
SSD_STATE = 128
SSD_CONV = 4
SSD_CHUNK = 128
TOP_K = 4
SWIGLU_ALPHA = 1.702
SWIGLU_LIMIT = 7.0
N_MOD = 6
NORM_EPS = 1e-6

LANES = 128
SUBLANES = 8
VMEM_LIMIT_BYTES = 56 * 1024 * 1024

GLUE_ROWS = 128
MOE_TILE_ROWS = 256
GATHER_UNROLL = 8
DMA_PRIORITIES = 2

F32 = jnp.float32
BF16 = jnp.bfloat16
HIGHEST = lax.Precision.HIGHEST
NEG_INF = float("-inf")
LOG2E = math.log2(math.e)
F32_UNDERFLOW_LOG2 = -160.0


def _params(sem):
    return pltpu.CompilerParams(dimension_semantics=sem, vmem_limit_bytes=VMEM_LIMIT_BYTES)


def _sigmoid(x):
    return 1.0 / (1.0 + jnp.exp(-x))


def _softplus(x):
    return jnp.maximum(x, 0.0) + jnp.log(1.0 + jnp.exp(-jnp.abs(x)))


def _tile(n, pref):
    if n <= pref:
        return n
    t = pref
    while n % t:
        t //= 2
    return t


def _cond_kernel(c_ref, w_ref, b_ref, o_ref):
    c = c_ref[...]
    s = c * _sigmoid(c)
    r = jnp.dot(s, w_ref[...], precision=HIGHEST, preferred_element_type=F32) + b_ref[...]
    o_ref[...] = r * _sigmoid(r)


def _mod_kernel(cond_ref, w_ref, b_ref, o_ref):
    o_ref[0] = jnp.dot(cond_ref[...], w_ref[0], precision=HIGHEST,
                       preferred_element_type=F32) + b_ref[0]


def _conditioning(c, w_cond, b_cond, w_mod, b_mod):
    bsz, d = c.shape
    rank = w_cond.shape[1]
    depth, _, nmod = w_mod.shape
    rows = SUBLANES
    c8 = jnp.broadcast_to(c[:1], (rows, d))
    cond = pl.pallas_call(
        _cond_kernel,
        out_shape=jax.ShapeDtypeStruct((rows, rank), F32),
        compiler_params=_params(None),
        name="cond_trunk",
    )(c8, w_cond, b_cond.reshape(1, rank))
    tn = _tile(nmod, 4096)
    mod = pl.pallas_call(
        _mod_kernel,
        grid=(depth, nmod // tn),
        in_specs=[
            pl.BlockSpec((rows, rank), lambda i, j: (0, 0)),
            pl.BlockSpec((1, rank, tn), lambda i, j: (i, 0, j)),
            pl.BlockSpec((1, 1, tn), lambda i, j: (i, 0, j)),
        ],
        out_specs=pl.BlockSpec((1, rows, tn), lambda i, j: (i, 0, j)),
        out_shape=jax.ShapeDtypeStruct((depth, rows, nmod), F32),
        compiler_params=_params(("arbitrary", "arbitrary")),
        name="cond_mod",
    )(cond, w_mod, b_mod.reshape(depth, 1, nmod))
    return mod[:, 0, :]


def _rms(y):
    return y * lax.rsqrt(jnp.mean(y * y, axis=-1, keepdims=True) + NORM_EPS)


def _pack_bf16_pairs(v):
    half = v.shape[1] // 2
    bits = lax.bitcast_convert_type(v.astype(BF16).astype(F32), jnp.uint32)
    return (bits[:, :half] >> 16) | bits[:, half:]


def _unpack_bf16_pairs(w):
    lo = lax.bitcast_convert_type(w << 16, F32)
    hi = lax.bitcast_convert_type(w & jnp.uint32(0xFFFF0000), F32)
    return jnp.concatenate([lo, hi], axis=1)


def _row_copy(src_hbm, row, dst, r, sem):
    return pltpu.make_async_copy(src_hbm.at[pl.ds(row, 1)], dst.at[pl.ds(r, 1)], sem)


def _gather_start(src_hbm, idx_ref, base, n, dst, sem):
    def body(r, carry):
        _row_copy(src_hbm, idx_ref[base + r], dst, r, sem).start()
        return carry

    lax.fori_loop(0, n, body, 0, unroll=GATHER_UNROLL)


def _gather_wait(src_hbm, n, dst, sem):
    def body(r, carry):
        _row_copy(src_hbm, 0, dst, r, sem).wait()
        return carry

    lax.fori_loop(0, n, body, 0, unroll=GATHER_UNROLL)


def _glue_kernel(*refs, has_resid, has_next, has_router, gathered):
    it = iter(refs)
    if gathered:
        pos_ref = next(it)
    x_ref = next(it)
    if gathered:
        rw_ref = next(it)
    if has_resid:
        y_ref, ga_ref, gpost_ref = next(it), next(it), next(it)
    if has_next:
        gpre_ref, sc_ref, sh_ref = next(it), next(it), next(it)
    if has_router:
        wr_ref, br_ref = next(it), next(it)
    if has_resid:
        xo_ref = next(it)
    if has_next:
        h_ref = next(it)
    if has_router:
        route_ref = next(it)
    if gathered:
        buf, sem = next(it), next(it)

    x = x_ref[...]
    if has_resid:
        if gathered:
            i = pl.program_id(0)
            tm = x.shape[0]
            nrow = TOP_K * tm
            slot = i % 2

            @pl.when(i == 0)
            def _():
                _gather_start(y_ref, pos_ref, 0, nrow, buf.at[0], sem.at[0])

            _gather_wait(y_ref, nrow, buf.at[slot], sem.at[slot])
            last = pl.num_programs(0) - 1
            nxt_base = jnp.minimum(i + 1, last) * nrow
            for r in range(nrow):
                _row_copy(y_ref, pos_ref[nxt_base + r], buf.at[1 - slot], r,
                          sem.at[1 - slot]).start(priority=r % DMA_PRIORITIES)
            rw = rw_ref[...]
            y = rw[:, TOP_K:TOP_K + 1] * _unpack_bf16_pairs(buf[slot, 0:tm, :])
            for k in range(1, TOP_K):
                y = y + (rw[:, TOP_K + k:TOP_K + k + 1]
                         * _unpack_bf16_pairs(buf[slot, k * tm:(k + 1) * tm, :]))
        else:
            y = y_ref[...]
        x = x + ga_ref[...] * (_rms(y) * gpost_ref[...])
        xo_ref[...] = x
    if has_next:
        h = (_rms(x) * gpre_ref[...]) * (1.0 + sc_ref[...]) + sh_ref[...]
        if has_router:
            h_ref[...] = _pack_bf16_pairs(h)
        else:
            h_ref[...] = h.astype(h_ref.dtype)
    if has_router:
        h_hi = h.astype(BF16)
        h_lo = (h - h_hi.astype(F32)).astype(BF16)
        w_hi, w_lo = wr_ref[0], wr_ref[1]
        logits = (jnp.dot(h_hi, w_hi, preferred_element_type=F32)
                  + (jnp.dot(h_lo, w_hi, preferred_element_type=F32)
                     + jnp.dot(h_hi, w_lo, preferred_element_type=F32))) + br_ref[...]
        lane = lax.broadcasted_iota(jnp.int32, logits.shape, 1).astype(F32)
        work = logits
        vals, idxs = [], []
        for _ in range(TOP_K):
            m = jnp.max(work, axis=1, keepdims=True)
            idx = jnp.min(jnp.where(work == m, lane, float(LANES)), axis=1, keepdims=True)
            vals.append(m)
            idxs.append(idx)
            work = jnp.where(lane == idx, NEG_INF, work)
        exps = [jnp.exp(v - vals[0]) for v in vals]
        denom = exps[0]
        for e in exps[1:]:
            denom = denom + e
        route = jnp.zeros_like(logits)
        for j in range(TOP_K):
            route = jnp.where(lane == float(j), idxs[j], route)
            route = jnp.where(lane == float(TOP_K + j), exps[j] / denom, route)
        route_ref[...] = route
    if gathered:
        @pl.when(i == last)
        def _():
            _gather_wait(y_ref, nrow, buf.at[1 - slot], sem.at[1 - slot])


def _glue(x, resid=None, nxt=None, router=None, gather=None, tm=GLUE_ROWS):
    t, d = x.shape
    tm = _tile(t, tm)
    gathered = gather is not None
    row = pl.BlockSpec((tm, d), lambda i, *_: (i, 0))
    vec = pl.BlockSpec((1, d), lambda i, *_: (0, 0))
    lanes = pl.BlockSpec((tm, LANES), lambda i, *_: (i, 0))
    args, in_specs, out_shape, out_specs, scratch = [x], [row], [], [], []
    if gathered:
        args.append(gather[1])
        in_specs.append(lanes)
    if resid is not None:
        y, ga, gpost = resid
        args += [y, ga.reshape(1, d), gpost.reshape(1, d)]
        in_specs += [pl.BlockSpec(memory_space=pl.ANY) if gathered else row, vec, vec]
        out_shape.append(jax.ShapeDtypeStruct((t, d), F32))
        out_specs.append(row)
    if nxt is not None:
        gpre, sc, sh = nxt
        args += [gpre.reshape(1, d), sc.reshape(1, d), sh.reshape(1, d)]
        in_specs += [vec, vec, vec]
        if router is None:
            out_shape.append(jax.ShapeDtypeStruct((t, d), BF16))
            out_specs.append(row)
        else:
            out_shape.append(jax.ShapeDtypeStruct((t, d // 2), jnp.uint32))
            out_specs.append(pl.BlockSpec((tm, d // 2), lambda i, *_: (i, 0)))
    if router is not None:
        wr, br = router
        args += [wr, br]
        in_specs += [pl.BlockSpec((2, d, LANES), lambda i, *_: (0, 0, 0)),
                     pl.BlockSpec((1, LANES), lambda i, *_: (0, 0))]
        out_shape.append(jax.ShapeDtypeStruct((t, LANES), F32))
        out_specs.append(lanes)
    if gathered:
        args = [gather[0]] + args
        scratch = [pltpu.VMEM((2, TOP_K * tm, d // 2), jnp.uint32),
                   pltpu.SemaphoreType.DMA((2,))]
    outs = pl.pallas_call(
        functools.partial(_glue_kernel, has_resid=resid is not None, has_next=nxt is not None,
                          has_router=router is not None, gathered=gathered),
        grid_spec=pltpu.PrefetchScalarGridSpec(
            num_scalar_prefetch=1 if gathered else 0,
            grid=(t // tm,),
            in_specs=in_specs,
            out_specs=out_specs,
            scratch_shapes=scratch,
        ),
        out_shape=out_shape,
        compiler_params=_params(("arbitrary",)),
        name="glue_gather" if gathered else "glue",
    )(*args)
    return outs


def _mm_kernel(a_ref, w_ref, o_ref, *scratch, nk, scale):
    part = jnp.dot(a_ref[...], w_ref[0].astype(BF16), preferred_element_type=F32)
    if nk == 1:
        if scale != 1.0:
            part = part * scale
        o_ref[...] = part.astype(o_ref.dtype)
        return
    acc_ref = scratch[0] if scratch else o_ref
    k = pl.program_id(2)

    @pl.when(k == 0)
    def _():
        acc_ref[...] = part

    @pl.when(k > 0)
    def _():
        acc_ref[...] += part

    if scratch or scale != 1.0:
        @pl.when(k == nk - 1)
        def _():
            r = acc_ref[...]
            if scale != 1.0:
                r = r * scale
            o_ref[...] = r.astype(o_ref.dtype)


def _matmul(a, w, layer, out_dtype, col0=0, ncols=None, scale=1.0, tm=1024, tn=512, tk=4096):
    m, k = a.shape
    n = w.shape[2] - col0 if ncols is None else ncols
    tm, tn, tk = _tile(m, tm), _tile(n, tn), _tile(k, tk)
    assert col0 % tn == 0 and n % tn == 0
    nk = k // tk
    jb = col0 // tn
    scratch = []
    if nk > 1 and out_dtype != F32:
        scratch = [pltpu.VMEM((tm, tn), F32)]
    return pl.pallas_call(
        functools.partial(_mm_kernel, nk=nk, scale=scale),
        grid=(m // tm, n // tn, nk),
        in_specs=[pl.BlockSpec((tm, tk), lambda i, j, kk: (i, kk)),
                  pl.BlockSpec((1, tk, tn), lambda i, j, kk: (layer, kk, jb + j))],
        out_specs=pl.BlockSpec((tm, tn), lambda i, j, kk: (i, j)),
        out_shape=jax.ShapeDtypeStruct((m, n), out_dtype),
        scratch_shapes=scratch,
        compiler_params=_params(("arbitrary", "arbitrary", "arbitrary")),
        name="matmul",
    )(a, w)


def _attn_kernel(q_ref, k_ref, v_ref, o_ref, acc_ref, carry_ref, *, tq, tk):
    i = pl.program_id(1)
    r2 = lax.broadcasted_iota(jnp.int32, (tk, tk), 0)
    c2 = lax.broadcasted_iota(jnp.int32, (tk, tk), 1)
    later = jnp.where(r2 > c2, 1.0, 0.0).astype(BF16)

    acc_ref[...] = jnp.zeros(acc_ref.shape, F32)
    carry_ref[...] = jnp.zeros(carry_ref.shape, F32)

    def span(k_start, nsub, r0, r1, diag):
        kk = k_ref[pl.ds(k_start, nsub * tk), :]
        vv = v_ref[pl.ds(k_start, nsub * tk), :]
        n = lax.dot_general(q_ref[r0:r1, :], kk, (((1,), (1,)), ((), ())),
                            preferred_element_type=F32)
        neg_abs = lax.bitcast_convert_type(
            lax.bitcast_convert_type(n, jnp.uint32) | jnp.uint32(0x80000000), F32)
        lk = jnp.minimum(n, 0.0) - jnp.log(1.0 + jnp.exp2(neg_abs)) * LOG2E
        if diag:
            row = lax.broadcasted_iota(jnp.int32, n.shape, 0)
            col = lax.broadcasted_iota(jnp.int32, n.shape, 1)
            vis = col < row
            lk = jnp.where(vis, lk, 0.0)
        carry = carry_ref[r0:r1, :]
        ws = [None] * nsub
        for s in range(nsub - 1, -1, -1):
            lk_s = lk[:, s * tk:(s + 1) * tk]
            la = jnp.dot(lk_s.astype(BF16), later, preferred_element_type=F32) + carry
            ws[s] = jnp.exp2((la + lk_s) - n[:, s * tk:(s + 1) * tk])
            carry = la[:, 0:1] + lk_s[:, 0:1]
        w = ws[0] if nsub == 1 else jnp.concatenate(ws, axis=1)
        if diag:
            w = jnp.where(vis, w, 0.0)
        acc_ref[r0:r1, :] += jnp.dot(w.astype(BF16), vv, preferred_element_type=F32)
        carry_ref[r0:r1, :] = carry

    q0 = i * tq
    for d in range(tq // tk - 1, -1, -1):
        span(pl.multiple_of(q0 + d * tk, tk), 1, d * tk, tq, True)

    nsub = 1
    nstep = q0 // (nsub * tk)

    def more(state):
        t, top = state
        return jnp.logical_and(t < nstep, top > F32_UNDERFLOW_LOG2)

    def body(state):
        t, _ = state
        span(pl.multiple_of((nstep - 1 - t) * (nsub * tk), nsub * tk), nsub, 0, tq, False)
        return t + 1, jnp.max(carry_ref[...])

    lax.while_loop(more, body, (jnp.int32(0), jnp.max(carry_ref[...])))
    o_ref[...] = acc_ref[...].astype(o_ref.dtype)


def _attention(q, kv, tq=1024, tk=256):
    l, d = q.shape
    nh = d // SB_HEAD_DIM
    tq = _tile(l, tq)
    tk = _tile(tq, tk)
    return pl.pallas_call(
        functools.partial(_attn_kernel, tq=tq, tk=tk),
        grid=(nh, l // tq),
        in_specs=[
            pl.BlockSpec((tq, SB_HEAD_DIM), lambda h, i: (i, h)),
            pl.BlockSpec((l, SB_HEAD_DIM), lambda h, i: (0, h)),
            pl.BlockSpec((l, SB_HEAD_DIM), lambda h, i: (0, nh + h)),
        ],
        out_specs=pl.BlockSpec((tq, SB_HEAD_DIM), lambda h, i: (i, h)),
        out_shape=jax.ShapeDtypeStruct((l, d), BF16),
        scratch_shapes=[pltpu.VMEM((tq, SB_HEAD_DIM), F32), pltpu.VMEM((tq, 1), F32)],
        compiler_params=_params(("arbitrary", "arbitrary")),
        name="sb_attention",
    )(q, kv, kv)


def _conv_kernel(u_ref, w_ref, b_ref, o_ref, ext_ref, *, tl):
    halo = SUBLANES

    @pl.when(pl.program_id(1) == 0)
    def _():
        ext_ref[0:halo, :] = jnp.zeros((halo, ext_ref.shape[1]), F32)

    u = u_ref[...]
    ext_ref[halo:halo + tl, :] = u
    acc = b_ref[...] + w_ref[SSD_CONV - 1:SSD_CONV, :] * u
    for k in range(SSD_CONV - 1):
        off = halo - (SSD_CONV - 1) + k
        acc = acc + w_ref[k:k + 1, :] * ext_ref[off:off + tl, :]
    o_ref[...] = acc * _sigmoid(acc)
    ext_ref[0:halo, :] = u[tl - halo:tl, :]


def _conv_silu(u, w, b, tl=512, tc=1024):
    l, c = u.shape
    tl, tc = _tile(l, tl), _tile(c, tc)
    return pl.pallas_call(
        functools.partial(_conv_kernel, tl=tl),
        grid=(c // tc, l // tl),
        in_specs=[
            pl.BlockSpec((tl, tc), lambda j, i: (i, j)),
            pl.BlockSpec((SSD_CONV, tc), lambda j, i: (0, j)),
            pl.BlockSpec((1, tc), lambda j, i: (0, j)),
        ],
        out_specs=pl.BlockSpec((tl, tc), lambda j, i: (i, j)),
        out_shape=jax.ShapeDtypeStruct((l, c), F32),
        scratch_shapes=[pltpu.VMEM((tl + SUBLANES, tc), F32)],
        compiler_params=_params(("arbitrary", "arbitrary")),
        name="ssd_conv",
    )(u, w, b.reshape(1, c))


def _split3(x):
    p1 = x.astype(BF16)
    r1 = x - p1.astype(F32)
    p2 = r1.astype(BF16)
    p3 = (r1 - p2.astype(F32)).astype(BF16)
    return p1, p2, p3


def _dtprep_kernel(dt_ref, bias_ref, alog_ref, dtr_ref, acsr_ref, acsc_ref, *, hpg):
    dt = _softplus(dt_ref[...] + bias_ref[...])
    adt = dt * (-jnp.exp(alog_ref[...]))
    n = dt.shape[0]
    r = lax.broadcasted_iota(jnp.int32, (n, n), 0)
    c = lax.broadcasted_iota(jnp.int32, (n, n), 1)
    upto = jnp.where(c <= r, 1.0, 0.0).astype(BF16)
    acs = sum(jnp.dot(upto, p, preferred_element_type=F32) for p in _split3(adt))
    dtr_ref[...] = dt.T
    acsr_ref[...] = acs.T
    for g in range(acsc_ref.shape[0]):
        acsc_ref[g] = acs[:, g * hpg:(g + 1) * hpg]


def _dtprep(dt_raw, dt_bias, a_log):
    l, nh = dt_raw.shape
    hpg = nh // SSD_GROUPS
    ch = SSD_CHUNK
    return pl.pallas_call(
        functools.partial(_dtprep_kernel, hpg=hpg),
        grid=(l // ch,),
        in_specs=[
            pl.BlockSpec((ch, nh), lambda i: (i, 0)),
            pl.BlockSpec((1, nh), lambda i: (0, 0)),
            pl.BlockSpec((1, nh), lambda i: (0, 0)),
        ],
        out_specs=[
            pl.BlockSpec((nh, ch), lambda i: (0, i)),
            pl.BlockSpec((nh, ch), lambda i: (0, i)),
            pl.BlockSpec((SSD_GROUPS, ch, hpg), lambda i: (0, i, 0)),
        ],
        out_shape=[
            jax.ShapeDtypeStruct((nh, l), F32),
            jax.ShapeDtypeStruct((nh, l), F32),
            jax.ShapeDtypeStruct((SSD_GROUPS, l, hpg), F32),
        ],
        compiler_params=_params(("arbitrary",)),
        name="ssd_dtprep",
    )(dt_raw, dt_bias.reshape(1, nh), a_log.reshape(1, nh))


def _ssd_kernel(xs_ref, b_ref, c_ref, dtr_ref, acsr_ref, acsc_ref, z_ref, dexp_ref, nw_ref,
                o_ref, st_ref, y_ref, *, npairs):
    ch = SSD_CHUNK
    hp = SSD_HEAD_DIM

    @pl.when(pl.program_id(1) == 0)
    def _():
        st_ref[...] = jnp.zeros(st_ref.shape, F32)

    bm = b_ref[...]
    cm = c_ref[...]
    cb = lax.dot_general(cm.astype(BF16), bm.astype(BF16), (((1,), (1,)), ((), ())),
                         preferred_element_type=F32)
    bt = bm.T
    row = lax.broadcasted_iota(jnp.int32, (ch, ch), 0)
    col = lax.broadcasted_iota(jnp.int32, (ch, ch), 1)
    causal = col <= row
    first = lax.broadcasted_iota(jnp.int32, (ch, 2 * hp), 1) < hp
    dtr = dtr_ref[...]
    acsr = acsr_ref[...]
    acsc = acsc_ref[0]

    for pr in range(npairs):
        sl = slice(pr * 2 * hp, (pr + 1) * 2 * hp)
        xs_b = xs_ref[:, sl].astype(BF16)
        prev = st_ref[pr]
        rhs = jnp.concatenate([xs_b, prev.astype(BF16)], axis=0)
        outs, news = [], []
        for hh in (2 * pr, 2 * pr + 1):
            a_col = jnp.broadcast_to(acsc[:, hh:hh + 1], (ch, ch))
            a_row = acsr[hh:hh + 1, :]
            dt_row = dtr[hh:hh + 1, :]
            lmat = jnp.exp(jnp.where(causal, a_col - a_row, NEG_INF))
            scores = cb * lmat * dt_row
            c_dec = cm * jnp.exp(a_col)
            lhs = jnp.concatenate([scores.astype(BF16), c_dec.astype(BF16)], axis=1)
            outs.append(jnp.dot(lhs, rhs, preferred_element_type=F32))
            a_last = a_row[:, ch - 1:ch]
            b_dec = (bt * (dt_row * jnp.exp(a_last - a_row))).astype(BF16)
            news.append(jnp.exp(a_last) * prev
                        + jnp.dot(b_dec, xs_b, preferred_element_type=F32))
        y_ref[:, sl] = jnp.where(first, outs[0], outs[1])
        st_ref[pr] = jnp.where(first, news[0], news[1])

    y = y_ref[...] + dexp_ref[...] * xs_ref[...]
    z = z_ref[...]
    yg = y * (z * _sigmoid(z))
    o_ref[...] = (_rms(yg) * nw_ref[...]).astype(o_ref.dtype)


def _ssd_scan(xbc, z, dtr, acsr, acsc, d_exp, norm_w):
    l, d_inner = z.shape
    gw = d_inner // SSD_GROUPS
    npairs = gw // (2 * SSD_HEAD_DIM)
    hpg = gw // SSD_HEAD_DIM
    ch = SSD_CHUNK
    nb = d_inner // SSD_STATE
    return pl.pallas_call(
        functools.partial(_ssd_kernel, npairs=npairs),
        grid=(SSD_GROUPS, l // ch),
        in_specs=[
            pl.BlockSpec((ch, gw), lambda g, c: (c, g)),
            pl.BlockSpec((ch, SSD_STATE), lambda g, c: (c, nb + g)),
            pl.BlockSpec((ch, SSD_STATE), lambda g, c: (c, nb + SSD_GROUPS + g)),
            pl.BlockSpec((hpg, ch), lambda g, c: (g, c)),
            pl.BlockSpec((hpg, ch), lambda g, c: (g, c)),
            pl.BlockSpec((1, ch, hpg), lambda g, c: (g, c, 0)),
            pl.BlockSpec((ch, gw), lambda g, c: (c, g)),
            pl.BlockSpec((1, gw), lambda g, c: (0, g)),
            pl.BlockSpec((1, gw), lambda g, c: (0, g)),
        ],
        out_specs=pl.BlockSpec((ch, gw), lambda g, c: (c, g)),
        out_shape=jax.ShapeDtypeStruct((l, d_inner), BF16),
        scratch_shapes=[pltpu.VMEM((npairs, SSD_STATE, 2 * SSD_HEAD_DIM), F32),
                        pltpu.VMEM((ch, gw), F32)],
        compiler_params=_params(("arbitrary", "arbitrary")),
        name="ssd_scan",
    )(xbc, xbc, xbc, dtr, acsr, acsc, z, d_exp, norm_w.reshape(1, d_inner))


def _lookup(table, idx):
    hit = idx[:, None] == jnp.arange(table.shape[0], dtype=jnp.int32)[None, :]
    return jnp.sum(jnp.where(hit, table[None, :], 0), axis=1)


def _route_tables(route, ne, tr, tm):
    t = route.shape[0]
    a = t * TOP_K
    eidx = route[:, :TOP_K].astype(jnp.int32).reshape(a)
    order = jnp.argsort(eidx, stable=True).astype(jnp.int32)
    rank = jnp.argsort(order).astype(jnp.int32)
    counts = jnp.sum(eidx[:, None] == jnp.arange(ne, dtype=jnp.int32)[None, :], axis=0,
                     dtype=jnp.int32)
    starts = jnp.cumsum(counts) - counts
    padded = (counts + tr - 1) // tr * tr
    pends = jnp.cumsum(padded)
    pstarts = pends - padded
    pos = rank + _lookup(pstarts - starts, eidx)
    n_tiles = (a + ne * tr) // tr
    used = (pends[-1] // tr).astype(jnp.int32)
    tile0 = jnp.arange(n_tiles, dtype=jnp.int32) * tr
    tile0_used = jnp.minimum(tile0, (used - 1) * tr)
    tile_e = jnp.sum(pends[None, :] <= tile0_used[:, None], axis=1, dtype=jnp.int32)
    src_base = jnp.clip(_lookup(starts - pstarts, tile_e) + tile0_used, 0, a)
    order_tok = jnp.concatenate([order // TOP_K, jnp.zeros((tr,), jnp.int32)])
    pos_blk = pos.reshape(t // tm, tm, TOP_K).transpose(0, 2, 1).reshape(a)
    return order_tok, tile_e, src_base, used.reshape(1), pos_blk


def _moe_kernel(tile_e_ref, base_ref, used_ref, tok_ref, h_hbm, wgu_ref, wd_ref, bgu_ref, bd_ref,
                sel_ref, o_ref, buf, sem, *, tr):
    t = pl.program_id(0)
    used = used_ref[0]
    slot = t % 2
    f2 = bgu_ref.shape[-1]

    @pl.when(t == 0)
    def _():
        _gather_start(h_hbm, tok_ref, base_ref[0], tr, buf.at[0], sem.at[0])

    @pl.when(t < used)
    def _():
        _gather_wait(h_hbm, tr, buf.at[slot], sem.at[slot])
        nxt_base = base_ref[jnp.minimum(t + 1, used - 1)]
        for r in range(tr):
            _row_copy(h_hbm, tok_ref[nxt_base + r], buf.at[1 - slot], r,
                      sem.at[1 - slot]).start(priority=r % DMA_PRIORITIES)
        gu = jnp.dot(_unpack_bf16_pairs(buf[slot]).astype(BF16), wgu_ref[0, 0].astype(BF16),
                     preferred_element_type=F32) + bgu_ref[0]
        xg = jnp.minimum(gu, SWIGLU_LIMIT)
        glu = xg * _sigmoid(SWIGLU_ALPHA * xg)
        lin = jnp.clip(gu, -SWIGLU_LIMIT, SWIGLU_LIMIT) + 1.0
        pair = glu * pltpu.roll(lin, f2 - 1, 1)
        act = jnp.dot(pair.astype(BF16), sel_ref[...], preferred_element_type=F32)
        o_ref[...] = _pack_bf16_pairs(jnp.dot(act.astype(BF16), wd_ref[0, 0].astype(BF16),
                                              preferred_element_type=F32) + bd_ref[0])

    @pl.when(t + 1 == used)
    def _():
        _gather_wait(h_hbm, tr, buf.at[1 - slot], sem.at[1 - slot])

    @pl.when(t >= used)
    def _():
        o_ref[...] = jnp.zeros(o_ref.shape, o_ref.dtype)


def _moe_sparse(h, w_gu, w_down, layer, b_gu, b_down, order_tok, tile_e, src_base, used, tr):
    _, ne, d, f2 = w_gu.shape
    f = f2 // 2
    n_tiles = tile_e.shape[0]
    sel = (jnp.arange(f2)[:, None] == 2 * jnp.arange(f)[None, :]).astype(BF16)
    return pl.pallas_call(
        functools.partial(_moe_kernel, tr=tr),
        grid_spec=pltpu.PrefetchScalarGridSpec(
            num_scalar_prefetch=4,
            grid=(n_tiles,),
            in_specs=[
                pl.BlockSpec(memory_space=pl.ANY),
                pl.BlockSpec((1, 1, d, f2), lambda i, te, *_: (layer, te[i], 0, 0)),
                pl.BlockSpec((1, 1, f, d), lambda i, te, *_: (layer, te[i], 0, 0)),
                pl.BlockSpec((1, 1, f2), lambda i, te, *_: (te[i], 0, 0)),
                pl.BlockSpec((1, 1, d), lambda i, te, *_: (te[i], 0, 0)),
                pl.BlockSpec((f2, f), lambda i, *_: (0, 0)),
            ],
            out_specs=pl.BlockSpec((tr, d // 2), lambda i, *_: (i, 0)),
            scratch_shapes=[pltpu.VMEM((2, tr, d // 2), jnp.uint32),
                            pltpu.SemaphoreType.DMA((2,))],
        ),
        out_shape=jax.ShapeDtypeStruct((n_tiles * tr, d // 2), jnp.uint32),
        compiler_params=_params(("arbitrary",)),
        name="moe_experts",
    )(tile_e, src_base, used, order_tok, h, w_gu, w_down, b_gu, b_down, sel)


def kernel(x, c, w_cond, b_cond, w_mod, b_mod, g_mix_pre, g_mix_post, g_ffn_pre, g_ffn_post, attn_w_in, attn_w_out, ssd_w_in, ssd_conv_w, ssd_conv_b, ssd_dt_bias, ssd_a_log, ssd_d, ssd_norm_w, ssd_w_out, moe_w_router, moe_b_router, moe_w_gu, moe_b_gu, moe_w_down, moe_b_down):
    bsz, seq, d = x.shape
    assert bsz == 1, "kernel is written for a single sequence"
    depth = w_mod.shape[0]
    ne = moe_w_router.shape[-1]
    ff = moe_w_down.shape[2]
    d_inner = ssd_w_out.shape[1]
    conv_dim = ssd_conv_w.shape[-1]

    mod = _conditioning(c, w_cond, b_cond, w_mod, b_mod)

    def mods(i):
        return [mod[i, k * d:(k + 1) * d] for k in range(N_MOD)]

    xt = x.reshape(seq, d)
    tm = _tile(seq, GLUE_ROWS)
    sh_m, sc_m, ga_m, sh_f, sc_f, ga_f = mods(0)
    (h,) = _glue(xt, nxt=(g_mix_pre[0], sc_m, sh_m), tm=tm)

    for i in range(depth):
        j = i // 2
        if i % 2 == 0:
            q = _matmul(h, attn_w_in, j, BF16, col0=0, ncols=d,
                        scale=-LOG2E * SB_HEAD_DIM ** -0.5)
            kv = _matmul(h, attn_w_in, j, BF16, col0=d, ncols=2 * d)
            o = _attention(q, kv)
            y = _matmul(o, attn_w_out, j, F32)
        else:
            z = _matmul(h, ssd_w_in, j, F32, col0=0, ncols=d_inner)
            xbc = _matmul(h, ssd_w_in, j, F32, col0=d_inner, ncols=conv_dim)
            dt_raw = _matmul(h, ssd_w_in, j, F32, col0=d_inner + conv_dim)
            xbc = _conv_silu(xbc, ssd_conv_w[j], ssd_conv_b[j])
            dtr, acsr, acsc = _dtprep(dt_raw, ssd_dt_bias[j], ssd_a_log[j])
            d_exp = jnp.repeat(ssd_d[j], SSD_HEAD_DIM).reshape(1, d_inner)
            yn = _ssd_scan(xbc, z, dtr, acsr, acsc, d_exp, ssd_norm_w[j])
            y = _matmul(yn, ssd_w_out, j, F32)

        wr = jnp.pad(moe_w_router[i], ((0, 0), (0, LANES - ne)))
        wr_hi = wr.astype(BF16)
        wr = jnp.stack([wr_hi, (wr - wr_hi.astype(F32)).astype(BF16)])
        br = jnp.pad(moe_b_router[i], (0, LANES - ne), constant_values=NEG_INF).reshape(1, LANES)
        xt, h, route = _glue(xt, resid=(y, ga_m, g_mix_post[i]),
                             nxt=(g_ffn_pre[i], sc_f, sh_f), router=(wr, br), tm=tm)

        order_tok, tile_e, src_base, used, pos_blk = _route_tables(route, ne, MOE_TILE_ROWS, tm)
        y_rows = _moe_sparse(h, moe_w_gu, moe_w_down, i, moe_b_gu[i].reshape(ne, 1, 2 * ff),
                             moe_b_down[i].reshape(ne, 1, d), order_tok, tile_e, src_base, used,
                             MOE_TILE_ROWS)

        ga_prev = ga_f
        if i + 1 < depth:
            sh_m, sc_m, ga_m, sh_f, sc_f, ga_f = mods(i + 1)
            xt, h = _glue(xt, resid=(y_rows, ga_prev, g_ffn_post[i]),
                          nxt=(g_mix_pre[i + 1], sc_m, sh_m), gather=(pos_blk, route), tm=tm)
        else:
            (xt,) = _glue(xt, resid=(y_rows, ga_prev, g_ffn_post[i]), gather=(pos_blk, route),
                          tm=tm)

    return xt.reshape(bsz, seq, d)
```
